```python
import math
import jax, jax.numpy as jnp
from jax import lax
import numpy as np

D_MODEL = 1024
BATCH = 2
SEQ = 8192
DEPTH = 2

EPS = 1e-6
CHUNK = 128
SSD_INNER = D_MODEL
SSD_HEADDIM = 64
SSD_HEADS = SSD_INNER // SSD_HEADDIM
SSD_GROUPS = 2
SSD_STATE = 128
SSD_CONV = 4
SSD_CONV_DIM = SSD_INNER + 2 * SSD_GROUPS * SSD_STATE
LRU_WIDTH = D_MODEL
LRU_BLOCKS = 4
LRU_BLOCK = LRU_WIDTH // LRU_BLOCKS
LRU_CONV = 4
LRU_C = 8.0
RET_HEADS = 4
RET_QK_DIM = D_MODEL // (2 * RET_HEADS)
RET_V_DIM = D_MODEL // RET_HEADS
RET_WIDTH = RET_HEADS * RET_V_DIM
ROPE_BASE = 10000.0
FFN_DIM = 3 * D_MODEL
FFN_CONV = 3
N_BRANCHES = 3
IN_SPLITS = (SSD_INNER, SSD_CONV_DIM, SSD_HEADS, LRU_WIDTH, LRU_WIDTH,
             RET_HEADS * RET_QK_DIM, RET_HEADS * RET_QK_DIM, RET_WIDTH, RET_WIDTH,
             N_BRANCHES * D_MODEL)
IN_PROJ_DIM = sum(IN_SPLITS)

kernel_name = 'hybrid_ssd_rglru_retention_gated_block'


def rmsnorm(x, g):
    x32 = x.astype(jnp.float32)
    y = x32 * lax.rsqrt(jnp.mean(x32 * x32, axis=-1, keepdims=True) + EPS)
    return (y * g.astype(jnp.float32)).astype(x.dtype)


def causal_dwconv(x, w, b):
    k = w.shape[0]
    y = lax.conv_general_dilated(
        x, w[:, None, :].astype(x.dtype), window_strides=(1,), padding=[(k - 1, 0)],
        dimension_numbers=('NWC', 'WIO', 'NWC'), feature_group_count=x.shape[-1])
    return y + b


def ssd_chunked(xdt, a, bm, cm):
    b, s, h, p = xdt.shape
    g, n = bm.shape[-2:]
    r = h // g
    c, L = s // CHUNK, CHUNK
    xdt = xdt.reshape(b, c, L, g, r, p)
    a = a.reshape(b, c, L, g, r)
    bm = bm.reshape(b, c, L, g, n)
    cm = cm.reshape(b, c, L, g, n)
    a_cs = jnp.cumsum(a, axis=2)
    causal = jnp.tril(jnp.ones((L, L), dtype=bool))
    seg = a_cs[:, :, :, None] - a_cs[:, :, None]
    decay = jnp.exp(jnp.where(causal[:, :, None, None], seg, -jnp.inf))
    cb = jnp.einsum('bclgn,bcsgn->bclsg', cm, bm)
    y_diag = jnp.einsum('bclsgr,bcsgrp->bclgrp', cb[..., None] * decay, xdt)
    state_decay = jnp.exp(a_cs[:, :, -1:] - a_cs)
    chunk_states = jnp.einsum('bclgn,bclgr,bclgrp->bcgrpn', bm, state_decay, xdt)
    chunk_decay = jnp.exp(a_cs[:, :, -1])

    def step(state, inp):
        st, dec = inp
        return state * dec[..., None, None] + st, state

    init = jnp.zeros((b, g, r, p, n), dtype=jnp.float32)
    _, s_prev = lax.scan(step, init, (jnp.moveaxis(chunk_states, 1, 0), jnp.moveaxis(chunk_decay, 1, 0)))
    s_prev = jnp.moveaxis(s_prev, 0, 1)
    y_off = jnp.einsum('bclgn,bcgrpn,bclgr->bclgrp', cm, s_prev, jnp.exp(a_cs))
    return (y_diag + y_off).reshape(b, s, h, p)


def ssd_mixer(z, xbc, dt_raw, conv_w, conv_b, dt_bias, a_log, d_skip, norm_w):
    b, s, _ = z.shape
    xbc = jax.nn.silu(causal_dwconv(xbc, conv_w, conv_b))
    xs, bm, cm = jnp.split(xbc, [SSD_INNER, SSD_INNER + SSD_GROUPS * SSD_STATE], axis=-1)
    xs = xs.reshape(b, s, SSD_HEADS, SSD_HEADDIM).astype(jnp.float32)
    bm = bm.reshape(b, s, SSD_GROUPS, SSD_STATE).astype(jnp.float32)
    cm = cm.reshape(b, s, SSD_GROUPS, SSD_STATE).astype(jnp.float32)
    dt = jax.nn.softplus(dt_raw.astype(jnp.float32) + dt_bias.astype(jnp.float32))
    a = -jnp.exp(a_log.astype(jnp.float32))
    y = ssd_chunked(xs * dt[..., None], dt * a, bm, cm)
    y = y + xs * d_skip.astype(jnp.float32)[:, None]
    y = y.reshape(b, s, SSD_INNER) * jax.nn.silu(z.astype(jnp.float32))
    return rmsnorm(y, norm_w).astype(z.dtype)


def _lru_combine(left, right):
    a1, b1 = left
    a2, b2 = right
    return a1 * a2, a2 * b1 + b2


def rglru_mixer(y_gate, xb, conv_w, conv_b, w_a, b_a, w_i, b_i, lam):
    b, s, _ = xb.shape
    xb = causal_dwconv(xb, conv_w, conv_b).astype(jnp.float32)
    xblk = xb.reshape(b, s, LRU_BLOCKS, LRU_BLOCK)
    r = jax.nn.sigmoid(jnp.einsum('bski,kij->bskj', xblk, w_a).reshape(b, s, LRU_WIDTH) + b_a)
    i = jax.nn.sigmoid(jnp.einsum('bski,kij->bskj', xblk, w_i).reshape(b, s, LRU_WIDTH) + b_i)
    log_a = -LRU_C * r * jax.nn.softplus(-lam.astype(jnp.float32))
    a = jnp.exp(log_a)
    u = jnp.sqrt(-jnp.expm1(2.0 * log_a)) * (i * xb)
    _, h = lax.associative_scan(_lru_combine, (a, u), axis=1)
    return (h * jax.nn.gelu(y_gate.astype(jnp.float32))).astype(y_gate.dtype)


def rotary(x):
    s, d = x.shape[1], x.shape[-1]
    inv = 1.0 / (ROPE_BASE ** jnp.linspace(0.0, 1.0, d // 2, dtype=jnp.float32))
    ang = jnp.arange(s, dtype=jnp.float32)[:, None] * inv[None]
    cos = jnp.cos(ang)[None, :, None]
    sin = jnp.sin(ang)[None, :, None]
    x1 = x[..., 0::2]
    x2 = x[..., 1::2]
    return jnp.stack([x1 * cos - x2 * sin, x1 * sin + x2 * cos], axis=-1).reshape(x.shape)


def retention_chunked(q, k, v, log_gamma):
    b, s, h, dk = q.shape
    dv = v.shape[-1]
    c, L = s // CHUNK, CHUNK
    q = q.reshape(b, c, L, h, dk)
    k = k.reshape(b, c, L, h, dk)
    v = v.reshape(b, c, L, h, dv)
    idx = jnp.arange(L, dtype=jnp.float32)
    diff = idx[:, None] - idx[None, :]
    dmask = jnp.where(diff[None] >= 0, jnp.exp(jnp.maximum(diff, 0.0)[None] * log_gamma[:, None, None]), 0.0)
    scores = jnp.einsum('bclhd,bcshd->bchls', q, k) * dmask[None, None]
    inner = jnp.einsum('bchls,bcshe->bclhe', scores, v)
    k_decay = jnp.exp((L - 1.0 - idx)[:, None] * log_gamma[None])
    chunk_kv = jnp.einsum('bclhd,lh,bclhe->bchde', k, k_decay, v)
    chunk_decay = jnp.exp(L * log_gamma)

    def step(state, kv):
        return state * chunk_decay[:, None, None] + kv, state

    init = jnp.zeros((b, h, dk, dv), dtype=jnp.float32)
    _, s_prev = lax.scan(step, init, jnp.moveaxis(chunk_kv, 1, 0))
    s_prev = jnp.moveaxis(s_prev, 0, 1)
    q_decay = jnp.exp((idx + 1.0)[:, None] * log_gamma[None])
    cross = jnp.einsum('bclhd,lh,bchde->bclhe', q, q_decay, s_prev)
    return (inner + cross).reshape(b, s, h, dv)


def retention_mixer(q, k, v, g):
    b, s, _ = q.shape
    qh = rotary(q.reshape(b, s, RET_HEADS, RET_QK_DIM).astype(jnp.float32))
    kh = rotary(k.reshape(b, s, RET_HEADS, RET_QK_DIM).astype(jnp.float32)) * (RET_QK_DIM ** -0.5)
    vh = v.reshape(b, s, RET_HEADS, RET_V_DIM).astype(jnp.float32)
    log_gamma = jnp.log(1.0 - jnp.exp2(-5.0 - jnp.arange(RET_HEADS, dtype=jnp.float32)))
    y = retention_chunked(qh, kh, vh, log_gamma)
    y = y * lax.rsqrt(jnp.mean(y * y, axis=-1, keepdims=True) + EPS)
    y = y.reshape(b, s, RET_WIDTH) * jax.nn.silu(g.astype(jnp.float32))
    return y.astype(q.dtype)


def conv_ffn(h, w_in, conv_w, conv_b, w_out):
    a, u = jnp.split(h @ w_in, 2, axis=-1)
    a = causal_dwconv(a, conv_w, conv_b)
    return (jax.nn.gelu(a) * u) @ w_out


def setup_inputs(seed: int = 0) -> dict:
    key = jax.random.key(seed)
    ks = jax.random.split(key, 24)
    f32 = jnp.float32
    nrm = lambda k, shape, scale: scale * jax.random.normal(k, shape, dtype=f32)
    x = jax.random.normal(ks[0], (BATCH, SEQ, D_MODEL), dtype=f32)
    norm_mix = 1.0 + nrm(ks[1], (DEPTH, D_MODEL), 0.02)
    w_in = nrm(ks[2], (DEPTH, D_MODEL, IN_PROJ_DIM), D_MODEL ** -0.5)
    ssd_conv_w = nrm(ks[3], (DEPTH, SSD_CONV, SSD_CONV_DIM), SSD_CONV ** -0.5)
    ssd_conv_b = nrm(ks[4], (DEPTH, SSD_CONV_DIM), 0.02)
    dt0 = jnp.exp(jax.random.uniform(ks[5], (DEPTH, SSD_HEADS), dtype=f32,
                                     minval=math.log(1e-3), maxval=math.log(1e-1)))
    ssd_dt_bias = dt0 + jnp.log(-jnp.expm1(-dt0))
    ssd_a_log = jnp.log(jax.random.uniform(ks[6], (DEPTH, SSD_HEADS), dtype=f32, minval=1.0, maxval=16.0))
    ssd_d = 1.0 + nrm(ks[7], (DEPTH, SSD_HEADS), 0.1)
    ssd_norm = 1.0 + nrm(ks[8], (DEPTH, SSD_INNER), 0.02)
    lru_conv_w = nrm(ks[9], (DEPTH, LRU_CONV, LRU_WIDTH), LRU_CONV ** -0.5)
    lru_conv_b = nrm(ks[10], (DEPTH, LRU_WIDTH), 0.02)
    lru_w_a = nrm(ks[11], (DEPTH, LRU_BLOCKS, LRU_BLOCK, LRU_BLOCK), LRU_BLOCK ** -0.5)
    lru_b_a = nrm(ks[12], (DEPTH, LRU_WIDTH), 0.02)
    lru_w_i = nrm(ks[13], (DEPTH, LRU_BLOCKS, LRU_BLOCK, LRU_BLOCK), LRU_BLOCK ** -0.5)
    lru_b_i = nrm(ks[14], (DEPTH, LRU_WIDTH), 0.02)
    a8 = jax.random.uniform(ks[15], (DEPTH, LRU_WIDTH), dtype=f32, minval=0.9, maxval=0.999)
    a0 = a8 ** (1.0 / LRU_C)
    lru_lambda = jnp.log(a0) - jnp.log1p(-a0)
    w_branch = nrm(ks[16], (DEPTH, N_BRANCHES, D_MODEL, D_MODEL), D_MODEL ** -0.5)
    w_o = nrm(ks[17], (DEPTH, D_MODEL, D_MODEL), D_MODEL ** -0.5)
    norm_ffn = 1.0 + nrm(ks[18], (DEPTH, D_MODEL), 0.02)
    ffn_w_in = nrm(ks[19], (DEPTH, D_MODEL, 2 * FFN_DIM), D_MODEL ** -0.5)
    ffn_conv_w = nrm(ks[20], (DEPTH, FFN_CONV, FFN_DIM), FFN_CONV ** -0.5)
    ffn_conv_b = nrm(ks[21], (DEPTH, FFN_DIM), 0.02)
    ffn_w_out = nrm(ks[22], (DEPTH, FFN_DIM, D_MODEL), FFN_DIM ** -0.5)
    norm_final = 1.0 + nrm(ks[23], (D_MODEL,), 0.02)
    return {'x': x, 'norm_mix': norm_mix, 'w_in': w_in,
            'ssd_conv_w': ssd_conv_w, 'ssd_conv_b': ssd_conv_b, 'ssd_dt_bias': ssd_dt_bias,
            'ssd_a_log': ssd_a_log, 'ssd_d': ssd_d, 'ssd_norm': ssd_norm,
            'lru_conv_w': lru_conv_w, 'lru_conv_b': lru_conv_b, 'lru_w_a': lru_w_a, 'lru_b_a': lru_b_a,
            'lru_w_i': lru_w_i, 'lru_b_i': lru_b_i, 'lru_lambda': lru_lambda,
            'w_branch': w_branch, 'w_o': w_o, 'norm_ffn': norm_ffn,
            'ffn_w_in': ffn_w_in, 'ffn_conv_w': ffn_conv_w, 'ffn_conv_b': ffn_conv_b,
            'ffn_w_out': ffn_w_out, 'norm_final': norm_final}


def reference(x, norm_mix, w_in, ssd_conv_w, ssd_conv_b, ssd_dt_bias, ssd_a_log, ssd_d, ssd_norm,
              lru_conv_w, lru_conv_b, lru_w_a, lru_b_a, lru_w_i, lru_b_i, lru_lambda,
              w_branch, w_o, norm_ffn, ffn_w_in, ffn_conv_w, ffn_conv_b, ffn_w_out, norm_final):
    b, s, _ = x.shape
    offsets = np.cumsum(IN_SPLITS)[:-1].tolist()
    for l in range(DEPTH):
        h = rmsnorm(x, norm_mix[l])
        proj = h @ w_in[l]
        (z, xbc, dt_raw, lru_y, lru_x, q, k, v, ret_g, gate_pre) = jnp.split(proj, offsets, axis=-1)
        y_ssd = ssd_mixer(z, xbc, dt_raw, ssd_conv_w[l], ssd_conv_b[l], ssd_dt_bias[l],
                          ssd_a_log[l], ssd_d[l], ssd_norm[l])
        y_lru = rglru_mixer(lru_y, lru_x, lru_conv_w[l], lru_conv_b[l], lru_w_a[l], lru_b_a[l],
                            lru_w_i[l], lru_b_i[l], lru_lambda[l])
        y_ret = retention_mixer(q, k, v, ret_g)
        branches = jnp.stack([y_ssd, y_lru, y_ret], axis=2)
        branch_d = jnp.einsum('bskw,kwd->bskd', branches, w_branch[l])
        gates = jax.nn.sigmoid(gate_pre.reshape(b, s, N_BRANCHES, D_MODEL))
        merged = jnp.sum(gates * branch_d, axis=2)
        x = x + merged @ w_o[l]
        h = rmsnorm(x, norm_ffn[l])
        x = x + conv_ffn(h, ffn_w_in[l], ffn_conv_w[l], ffn_conv_b[l], ffn_w_out[l])
    return rmsnorm(x, norm_final)
```

```python
import functools
import math

import jax
import jax.numpy as jnp
from jax import lax
from jax.experimental import pallas as pl
from jax.experimental.pallas import tpu as pltpu

F32 = jnp.float32
BF16 = jnp.bfloat16

EPS = 1e-6
CHUNK = 128
D = 1024
SSD_HEADS = 16
SSD_HEADDIM = 64
SSD_GROUPS = 2
SSD_STATE = 128
SSD_CONV = 4
SSD_XBC = D + 2 * SSD_GROUPS * SSD_STATE
LRU_BLOCKS = 4
LRU_BLOCK = D // LRU_BLOCKS
LRU_CONV = 4
LRU_C = 8.0
RET_HEADS = 4
RET_DK = 128
RET_DV = 256
ROPE_BASE = 10000.0
FFN_DIM = 3 * D
FFN_CONV = 3
LANES = 128
SUBLANES = 8
TAIL = SUBLANES
DT_SPLITS = 3
VMEM_LIMIT = 56 * 1024 * 1024

TS_MIX = 512
TS_FFN = 512
FFN_BLK = 1024


def _rmsnorm(x, g):
    return x * lax.rsqrt(jnp.mean(x * x, axis=-1, keepdims=True) + EPS) * g


def _mm(a, b):
    return jnp.dot(a.astype(BF16), b.astype(BF16), preferred_element_type=F32)


def _mm_tn(a, b):
    return lax.dot_general(a.astype(BF16), b.astype(BF16), (((0,), (0,)), ((), ())),
                           preferred_element_type=F32)


def _mm_nt(a, b):
    return lax.dot_general(a.astype(BF16), b.astype(BF16), (((1,), (1,)), ((), ())),
                           preferred_element_type=F32)


def _sigmoid(x):
    return jax.nn.sigmoid(x)


def _silu(x):
    return x * jax.nn.sigmoid(x)


def _gelu_tanh(x):
    c = math.sqrt(2.0 / math.pi)
    return 0.5 * x * (1.0 + jnp.tanh(c * (x + 0.044715 * (x * x * x))))


def _softplus(x):
    return jnp.maximum(x, 0.0) + jnp.log1p(jnp.exp(-jnp.abs(x)))


def _split3(v):
    s1 = v.astype(BF16).astype(F32)
    r1 = v - s1
    s2 = r1.astype(BF16).astype(F32)
    s3 = (r1 - s2).astype(BF16).astype(F32)
    return s1, s2, s3


def _conv_from_ext(ext_ref, w_ref, b_ref, taps, rows):
    y = ext_ref[TAIL:TAIL + rows, :] * w_ref[taps - 1:taps, :] + b_ref[...]
    for j in range(1, taps):
        y = y + ext_ref[TAIL - j:TAIL - j + rows, :] * w_ref[taps - 1 - j:taps - j, :]
    return y


def _const_spec(shape):
    nd = len(shape)
    return pl.BlockSpec(shape, lambda b, s: (0,) * nd, pipeline_mode=pl.Buffered(1))


def _tile_spec(ts, width):
    return pl.BlockSpec((None, ts, width), lambda b, s: (b, s, 0))


def _params():
    return pltpu.CompilerParams(dimension_semantics=("arbitrary", "arbitrary"),
                                vmem_limit_bytes=VMEM_LIMIT)


def _ffn_kernel(x_ref, g_ref, w_in_ref, cw_ref, cb_ref, w_out_ref, gf_ref, o_ref,
                ext_ref, tail_ref, *, final_norm):
    ts = x_ref.shape[0]
    s = pl.program_id(1)

    @pl.when(s == 0)
    def _():
        tail_ref[...] = jnp.zeros_like(tail_ref)

    x = x_ref[...]
    h = _rmsnorm(x, g_ref[...]).astype(BF16)
    acc = x
    for j in range(FFN_DIM // FFN_BLK):
        lo = j * FFN_BLK
        a = jnp.dot(h, w_in_ref[:, lo:lo + FFN_BLK], preferred_element_type=F32)
        u = jnp.dot(h, w_in_ref[:, FFN_DIM + lo:FFN_DIM + lo + FFN_BLK], preferred_element_type=F32)
        ext_ref[0:TAIL, :] = tail_ref[:, lo:lo + FFN_BLK]
        ext_ref[TAIL:TAIL + ts, :] = a
        tail_ref[:, lo:lo + FFN_BLK] = a[ts - TAIL:ts, :]
        ac = _conv_from_ext(ext_ref, cw_ref.at[:, lo:lo + FFN_BLK], cb_ref.at[:, lo:lo + FFN_BLK],
                            FFN_CONV, ts)
        act = (_gelu_tanh(ac) * u).astype(BF16)
        acc = acc + jnp.dot(act, w_out_ref[lo:lo + FFN_BLK, :], preferred_element_type=F32)
    if final_norm:
        acc = _rmsnorm(acc, gf_ref[...])
    o_ref[...] = acc


def _ffn_call(x, g, w_in, cw, cb, w_out, gf, final_norm):
    bsz, seq, _ = x.shape
    ts = min(TS_FFN, seq)
    return pl.pallas_call(
        functools.partial(_ffn_kernel, final_norm=final_norm),
        grid=(bsz, seq // ts),
        in_specs=[_tile_spec(ts, D), _const_spec((1, D)), _const_spec((D, 2 * FFN_DIM)),
                  _const_spec((FFN_CONV, FFN_DIM)), _const_spec((1, FFN_DIM)),
                  _const_spec((FFN_DIM, D)), _const_spec((1, D))],
        out_specs=_tile_spec(ts, D),
        out_shape=jax.ShapeDtypeStruct(x.shape, F32),
        scratch_shapes=[pltpu.VMEM((TAIL + ts, FFN_BLK), F32), pltpu.VMEM((TAIL, FFN_DIM), F32)],
        compiler_params=_params(),
        name="ffn",
    )(x, g, w_in, cw, cb, w_out, gf)


def _lru_kernel(x_ref, m_ref, g_ref, w_ref, cw_ref, cb_ref, wa_ref, ba_ref, wi_ref, bi_ref,
                lam_ref, wbr_ref, o_ref, ext_ref, a_ref, u_ref, h_ref, carry_ref):
    ts = x_ref.shape[0]
    s = pl.program_id(1)

    @pl.when(s == 0)
    def _():
        ext_ref[0:TAIL, :] = jnp.zeros((TAIL, D), F32)
        carry_ref[...] = jnp.zeros_like(carry_ref)

    h = _rmsnorm(x_ref[...], g_ref[...]).astype(BF16)
    ext_ref[TAIL:TAIL + ts, :] = jnp.dot(h, w_ref[:, D:2 * D], preferred_element_type=F32)
    xb = _conv_from_ext(ext_ref, cw_ref, cb_ref, LRU_CONV, ts)
    ext_ref[0:TAIL, :] = ext_ref[ts:ts + TAIL, :]

    xb16 = xb.astype(BF16)
    r_parts, i_parts = [], []
    for k in range(LRU_BLOCKS):
        blk = xb16[:, k * LRU_BLOCK:(k + 1) * LRU_BLOCK]
        r_parts.append(jnp.dot(blk, wa_ref[k], preferred_element_type=F32))
        i_parts.append(jnp.dot(blk, wi_ref[k], preferred_element_type=F32))
    r = _sigmoid(jnp.concatenate(r_parts, axis=1) + ba_ref[...])
    gi = _sigmoid(jnp.concatenate(i_parts, axis=1) + bi_ref[...])
    log_a = (-LRU_C) * r * _softplus(-lam_ref[...])
    a = jnp.exp(log_a)
    u = jnp.sqrt(-jnp.tanh(log_a) * (1.0 + a * a)) * (gi * xb)

    row = lax.broadcasted_iota(jnp.int32, (ts, D), 0)
    for d in (1, 2, 4):
        a_sh = jnp.where(row >= d, pltpu.roll(a, d, axis=0), 1.0)
        u_sh = jnp.where(row >= d, pltpu.roll(u, d, axis=0), 0.0)
        u = a * u_sh + u
        a = a * a_sh
    a_ref[...] = a
    u_ref[...] = u

    def step(k, hprev):
        r0 = pl.multiple_of(k * SUBLANES, SUBLANES)
        hk = a_ref[pl.ds(r0, SUBLANES), :] * hprev + u_ref[pl.ds(r0, SUBLANES), :]
        h_ref[pl.ds(r0, SUBLANES), :] = hk
        return hk

    first = a_ref[0:SUBLANES, :] * carry_ref[...] + u_ref[0:SUBLANES, :]
    h_ref[0:SUBLANES, :] = first
    last = lax.fori_loop(1, ts // SUBLANES, step, first, unroll=4)
    carry_ref[...] = jnp.broadcast_to(last[SUBLANES - 1:SUBLANES, :], (SUBLANES, D))

    yg = jnp.dot(h, w_ref[:, 0:D], preferred_element_type=F32)
    y = (h_ref[...] * _gelu_tanh(yg)).astype(BF16)
    br = jnp.dot(y, wbr_ref[...], preferred_element_type=F32)
    gate = jnp.dot(h, w_ref[:, 2 * D:3 * D], preferred_element_type=F32)
    o_ref[...] = m_ref[...] + _sigmoid(gate) * br


def _lru_call(x, merged, g, w, cw, cb, wa, ba, wi, bi, lam, wbr):
    bsz, seq, _ = x.shape
    ts = min(TS_MIX, seq)
    return pl.pallas_call(
        _lru_kernel,
        grid=(bsz, seq // ts),
        in_specs=[_tile_spec(ts, D), _tile_spec(ts, D), _const_spec((1, D)), _const_spec((D, 3 * D)),
                  _const_spec((LRU_CONV, D)), _const_spec((1, D)),
                  _const_spec((LRU_BLOCKS, LRU_BLOCK, LRU_BLOCK)), _const_spec((1, D)),
                  _const_spec((LRU_BLOCKS, LRU_BLOCK, LRU_BLOCK)), _const_spec((1, D)),
                  _const_spec((1, D)), _const_spec((D, D))],
        out_specs=_tile_spec(ts, D),
        out_shape=jax.ShapeDtypeStruct(x.shape, F32),
        scratch_shapes=[pltpu.VMEM((TAIL + ts, D), F32), pltpu.VMEM((ts, D), F32),
                        pltpu.VMEM((ts, D), F32), pltpu.VMEM((ts, D), F32),
                        pltpu.VMEM((SUBLANES, D), F32)],
        compiler_params=_params(),
        name="lru",
    )(x, merged, g, w, cw, cb, wa, ba, wi, bi, lam, wbr)


def _ret_kernel(x_ref, m_ref, g_ref, w_ref, inv_ref, wbr_ref, wo_ref, o_ref,
                q_ref, k_ref, v_ref, y_ref, state_ref):
    ts = x_ref.shape[0]
    s = pl.program_id(1)

    @pl.when(s == 0)
    def _():
        state_ref[...] = jnp.zeros_like(state_ref)

    x = x_ref[...]
    h = _rmsnorm(x, g_ref[...]).astype(BF16)
    qk_w = RET_HEADS * RET_DK
    q = jnp.dot(h, w_ref[:, 0:qk_w], preferred_element_type=F32)
    k = jnp.dot(h, w_ref[:, qk_w:2 * qk_w], preferred_element_type=F32)
    v_ref[...] = jnp.dot(h, w_ref[:, 2 * qk_w:2 * qk_w + D], preferred_element_type=F32)

    pos = (s * ts + lax.broadcasted_iota(jnp.int32, (ts, LANES), 0)).astype(F32)
    ang = pos * inv_ref[...]
    lane = lax.broadcasted_iota(jnp.int32, (ts, LANES), 1)
    even = (lane % 2) == 0
    cos_t = jnp.cos(ang)
    sin_t = jnp.where(even, -jnp.sin(ang), jnp.sin(ang))

    def rotary(t):
        swapped = jnp.where(even, pltpu.roll(t, LANES - 1, axis=1), pltpu.roll(t, 1, axis=1))
        return t * cos_t + swapped * sin_t

    for hd in range(RET_HEADS):
        sl = slice(hd * RET_DK, (hd + 1) * RET_DK)
        q_ref[:, sl] = rotary(q[:, sl])
        k_ref[:, sl] = rotary(k[:, sl]) * (RET_DK ** -0.5)

    li = lax.broadcasted_iota(jnp.int32, (CHUNK, CHUNK), 0)
    si = lax.broadcasted_iota(jnp.int32, (CHUNK, CHUNK), 1)
    diff = (li - si).astype(F32)
    idx = li.astype(F32)
    log_gamma = [math.log(1.0 - 2.0 ** (-5.0 - hd)) for hd in range(RET_HEADS)]
    dmask = [jnp.where(diff >= 0.0, jnp.exp(jnp.maximum(diff, 0.0) * lg), 0.0) for lg in log_gamma]
    q_decay = [jnp.exp((idx + 1.0) * lg) for lg in log_gamma]
    k_decay = [jnp.exp((CHUNK - 1.0 - idx) * lg) for lg in log_gamma]
    chunk_decay = [math.exp(CHUNK * lg) for lg in log_gamma]

    def chunk_step(c, carry):
        r0 = pl.multiple_of(c * CHUNK, CHUNK)
        outs = []
        for hd in range(RET_HEADS):
            qh = q_ref[pl.ds(r0, CHUNK), hd * RET_DK:(hd + 1) * RET_DK]
            kh = k_ref[pl.ds(r0, CHUNK), hd * RET_DK:(hd + 1) * RET_DK]
            vh = v_ref[pl.ds(r0, CHUNK), hd * RET_DV:(hd + 1) * RET_DV].astype(BF16)
            st = state_ref[hd]
            scores = _mm_nt(qh, kh) * dmask[hd]
            lhs = jnp.concatenate([scores.astype(BF16), (qh * q_decay[hd]).astype(BF16)], axis=1)
            rhs = jnp.concatenate([vh, st.astype(BF16)], axis=0)
            out = jnp.dot(lhs, rhs, preferred_element_type=F32)
            state_ref[hd] = st * chunk_decay[hd] + _mm_tn(kh * k_decay[hd], vh)
            outs.append(out * lax.rsqrt(jnp.mean(out * out, axis=-1, keepdims=True) + EPS))
        y_ref[pl.ds(r0, CHUNK), :] = jnp.concatenate(outs, axis=1)
        return carry

    lax.fori_loop(0, ts // CHUNK, chunk_step, 0)

    gr = jnp.dot(h, w_ref[:, 2 * qk_w + D:2 * qk_w + 2 * D], preferred_element_type=F32)
    y = (y_ref[...] * _silu(gr)).astype(BF16)
    br = jnp.dot(y, wbr_ref[...], preferred_element_type=F32)
    gate = jnp.dot(h, w_ref[:, 2 * qk_w + 2 * D:2 * qk_w + 3 * D], preferred_element_type=F32)
    merged = (m_ref[...] + _sigmoid(gate) * br).astype(BF16)
    o_ref[...] = x + jnp.dot(merged, wo_ref[...], preferred_element_type=F32)


def _ret_call(x, merged, g, w, inv, wbr, wo):
    bsz, seq, _ = x.shape
    ts = min(TS_MIX, seq)
    wcols = w.shape[1]
    return pl.pallas_call(
        _ret_kernel,
        grid=(bsz, seq // ts),
        in_specs=[_tile_spec(ts, D), _tile_spec(ts, D), _const_spec((1, D)), _const_spec((D, wcols)),
                  _const_spec((1, LANES)), _const_spec((D, D)), _const_spec((D, D))],
        out_specs=_tile_spec(ts, D),
        out_shape=jax.ShapeDtypeStruct(x.shape, F32),
        scratch_shapes=[pltpu.VMEM((ts, RET_HEADS * RET_DK), F32),
                        pltpu.VMEM((ts, RET_HEADS * RET_DK), F32),
                        pltpu.VMEM((ts, D), F32), pltpu.VMEM((ts, D), F32),
                        pltpu.VMEM((RET_HEADS, RET_DK, RET_DV), F32)],
        compiler_params=_params(),
        name="ret",
    )(x, merged, g, w, inv, wbr, wo)


def _ssd_kernel(x_ref, g_ref, w_ref, wdt_ref, cw_ref, cb_ref, dtb_ref, alog_ref, dskip_ref,
                nw_ref, e1_ref, e2_ref, tri_ref, wbr_ref, o_ref,
                ext_ref, xbc_ref, z_ref, dt_ref, y_ref, state_ref):
    ts = x_ref.shape[0]
    s = pl.program_id(1)

    @pl.when(s == 0)
    def _():
        ext_ref[0:TAIL, :] = jnp.zeros((TAIL, SSD_XBC), F32)
        state_ref[...] = jnp.zeros_like(state_ref)

    h = _rmsnorm(x_ref[...], g_ref[...]).astype(BF16)
    z_ref[...] = jnp.dot(h, w_ref[:, 0:D], preferred_element_type=F32)
    ext_ref[TAIL:TAIL + ts, :] = jnp.dot(h, w_ref[:, D:D + SSD_XBC], preferred_element_type=F32)
    xbc_ref[...] = _silu(_conv_from_ext(ext_ref, cw_ref, cb_ref, SSD_CONV, ts))
    ext_ref[0:TAIL, :] = ext_ref[ts:ts + TAIL, :]
    dt_ref[...] = _softplus(jnp.dot(h, wdt_ref[...], preferred_element_type=F32) + dtb_ref[...])

    a_head = -jnp.exp(alog_ref[...])
    lane = lax.broadcasted_iota(jnp.int32, (CHUNK, LANES), 1)
    li = lax.broadcasted_iota(jnp.int32, (CHUNK, CHUNK), 0)
    si = lax.broadcasted_iota(jnp.int32, (CHUNK, CHUNK), 1)
    causal = li >= si
    lo_half = lane < SSD_HEADDIM
    gw = SSD_STATE
    hp = D // SSD_GROUPS

    def pack3(v):
        s1, s2, s3 = _split3(v)
        return jnp.where(lane < SSD_HEADS, s1, jnp.where(lane < 2 * SSD_HEADS, s2, s3)).astype(BF16)

    def chunk_step(c, carry):
        r0 = pl.multiple_of(c * CHUNK, CHUNK)
        dt = dt_ref[pl.ds(r0, CHUNK), :]
        a = dt * a_head
        s1, s2, s3 = _split3(a)
        tri = tri_ref[...]
        acs = (jnp.dot(tri, s1.astype(BF16), preferred_element_type=F32)
               + jnp.dot(tri, s2.astype(BF16), preferred_element_type=F32)
               + jnp.dot(tri, s3.astype(BF16), preferred_element_type=F32))
        ea = jnp.exp(acs)
        wst = jnp.exp(acs[CHUNK - 1:CHUNK, :] - acs) * dt
        exp1 = jnp.dot(jnp.concatenate([pack3(ea), pack3(wst)], axis=0), e1_ref[...],
                       preferred_element_type=F32)
        ea_x = exp1[0:CHUNK, :]
        wst_x = exp1[CHUNK:2 * CHUNK, :]
        acs_x = jnp.dot(pack3(acs), e2_ref[...], preferred_element_type=F32)
        acs_t = acs.T
        dt_t = dt.T

        xs = xbc_ref[pl.ds(r0, CHUNK), 0:D]
        bm = xbc_ref[pl.ds(r0, CHUNK), D:D + SSD_GROUPS * gw]
        cm = xbc_ref[pl.ds(r0, CHUNK), D + SSD_GROUPS * gw:D + 2 * SSD_GROUPS * gw]
        xs16 = xs.astype(BF16)
        xw16 = (xs * wst_x).astype(BF16)

        y_parts = []
        for g in range(SSD_GROUPS):
            bg = bm[:, g * gw:(g + 1) * gw]
            cg = cm[:, g * gw:(g + 1) * gw]
            cb = _mm_nt(cg, bg)
            st = state_ref[:, g * hp:(g + 1) * hp]
            y_off = _mm(cg, st) * ea_x[:, g * hp:(g + 1) * hp]
            dec_last = ea_x[CHUNK - 1:CHUNK, g * hp:(g + 1) * hp]
            state_ref[:, g * hp:(g + 1) * hp] = st * dec_last + _mm_tn(bg, xw16[:, g * hp:(g + 1) * hp])
            heads_per_group = SSD_HEADS // SSD_GROUPS
            diag_parts = []
            for pr in range(heads_per_group // 2):
                ms = []
                for hh in (2 * pr, 2 * pr + 1):
                    hd = g * heads_per_group + hh
                    seg = acs_x[:, hd * LANES:(hd + 1) * LANES] - acs_t[hd:hd + 1, :]
                    dec = jnp.exp(jnp.where(causal, seg, -jnp.inf))
                    ms.append((cb * dec * dt_t[hd:hd + 1, :]).astype(BF16))
                col = g * hp + pr * LANES
                xp = xs16[:, col:col + LANES]
                zero = jnp.zeros_like(xp)
                rhs = jnp.concatenate([jnp.where(lo_half, xp, zero), jnp.where(lo_half, zero, xp)], axis=0)
                diag_parts.append(jnp.dot(jnp.concatenate(ms, axis=1), rhs, preferred_element_type=F32))
            y_parts.append(jnp.concatenate(diag_parts, axis=1) + y_off)
        y = jnp.concatenate(y_parts, axis=1) + xs * dskip_ref[...]
        y = y * _silu(z_ref[pl.ds(r0, CHUNK), :])
        y_ref[pl.ds(r0, CHUNK), :] = _rmsnorm(y, nw_ref[...])
        return carry

    lax.fori_loop(0, ts // CHUNK, chunk_step, 0)

    br = jnp.dot(y_ref[...].astype(BF16), wbr_ref[...], preferred_element_type=F32)
    gate = jnp.dot(h, w_ref[:, D + SSD_XBC:2 * D + SSD_XBC], preferred_element_type=F32)
    o_ref[...] = _sigmoid(gate) * br


def _ssd_call(x, g, w, wdt, cw, cb, dtb, alog, dskip, nw, e1, e2, tri, wbr):
    bsz, seq, _ = x.shape
    ts = min(TS_MIX, seq)
    wcols = w.shape[1]
    return pl.pallas_call(
        _ssd_kernel,
        grid=(bsz, seq // ts),
        in_specs=[_tile_spec(ts, D), _const_spec((1, D)), _const_spec((D, wcols)),
                  _const_spec((D, LANES)), _const_spec((SSD_CONV, SSD_XBC)), _const_spec((1, SSD_XBC)),
                  _const_spec((1, LANES)), _const_spec((1, LANES)), _const_spec((1, D)),
                  _const_spec((1, D)), _const_spec((LANES, D)), _const_spec((LANES, SSD_HEADS * LANES)),
                  _const_spec((CHUNK, CHUNK)), _const_spec((D, D))],
        out_specs=_tile_spec(ts, D),
        out_shape=jax.ShapeDtypeStruct(x.shape, F32),
        scratch_shapes=[pltpu.VMEM((TAIL + ts, SSD_XBC), F32), pltpu.VMEM((ts, SSD_XBC), F32),
                        pltpu.VMEM((ts, D), F32), pltpu.VMEM((ts, LANES), F32),
                        pltpu.VMEM((ts, D), F32), pltpu.VMEM((SSD_STATE, D), F32)],
        compiler_params=_params(),
        name="ssd",
    )(x, g, w, wdt, cw, cb, dtb, alog, dskip, nw, e1, e2, tri, wbr)


def _head_expand_matrix(width_per_head):
    r = jnp.arange(LANES)[:, None]
    c = jnp.arange(SSD_HEADS * width_per_head)[None, :]
    hit = (r < DT_SPLITS * SSD_HEADS) & ((c // width_per_head) == (r % SSD_HEADS))
    return hit.astype(BF16)


def _rep_heads(v):
    rep = jnp.tile(v.astype(F32), DT_SPLITS)
    return jnp.pad(rep, (0, LANES - rep.shape[0]))[None, :]


def kernel(x, norm_mix, w_in, ssd_conv_w, ssd_conv_b, ssd_dt_bias, ssd_a_log, ssd_d, ssd_norm,
           lru_conv_w, lru_conv_b, lru_w_a, lru_b_a, lru_w_i, lru_b_i, lru_lambda,
           w_branch, w_o, norm_ffn, ffn_w_in, ffn_conv_w, ffn_conv_b, ffn_w_out, norm_final):
    depth = w_in.shape[0]
    o_z, o_xbc, o_dt = 0, D, D + SSD_XBC
    o_ly = o_dt + SSD_HEADS
    o_lx, o_q = o_ly + D, o_ly + 2 * D
    o_k = o_q + RET_HEADS * RET_DK
    o_v = o_k + RET_HEADS * RET_DK
    o_rg, o_gate = o_v + D, o_v + 2 * D

    e1 = _head_expand_matrix(SSD_HEADDIM)
    e2 = _head_expand_matrix(LANES)
    tri = (jnp.arange(CHUNK)[:, None] >= jnp.arange(CHUNK)[None, :]).astype(BF16)
    inv = 1.0 / (ROPE_BASE ** jnp.linspace(0.0, 1.0, RET_DK // 2, dtype=F32))
    inv_row = jnp.repeat(inv, 2)[None, :]

    row = lambda v: v.astype(F32)[None, :]
    for l in range(depth):
        wl = w_in[l].astype(BF16)
        gates = [wl[:, o_gate + i * D:o_gate + (i + 1) * D] for i in range(3)]
        w_ssd = jnp.concatenate([wl[:, o_z:o_dt], gates[0]], axis=1)
        wdt = wl[:, o_dt:o_dt + SSD_HEADS]
        wdt = jnp.pad(jnp.tile(wdt, (1, DT_SPLITS)), ((0, 0), (0, LANES - DT_SPLITS * SSD_HEADS)))
        w_lru = jnp.concatenate([wl[:, o_ly:o_q], gates[1]], axis=1)
        w_ret = jnp.concatenate([wl[:, o_q:o_gate], gates[2]], axis=1)
        wbr = w_branch[l].astype(BF16)

        merged = _ssd_call(x, row(norm_mix[l]), w_ssd, wdt, ssd_conv_w[l], row(ssd_conv_b[l]),
                           _rep_heads(ssd_dt_bias[l]), _rep_heads(ssd_a_log[l]),
                           row(jnp.repeat(ssd_d[l], SSD_HEADDIM)), row(ssd_norm[l]), e1, e2, tri, wbr[0])
        merged = _lru_call(x, merged, row(norm_mix[l]), w_lru, lru_conv_w[l], row(lru_conv_b[l]),
                           lru_w_a[l].astype(BF16), row(lru_b_a[l]), lru_w_i[l].astype(BF16),
                           row(lru_b_i[l]), row(lru_lambda[l]), wbr[1])
        x = _ret_call(x, merged, row(norm_mix[l]), w_ret, inv_row, wbr[2], w_o[l].astype(BF16))
        x = _ffn_call(x, row(norm_ffn[l]), ffn_w_in[l].astype(BF16), ffn_conv_w[l], row(ffn_conv_b[l]),
                      ffn_w_out[l].astype(BF16), row(norm_final), final_norm=(l == depth - 1))
    return x
```

```python
import functools
import math

import jax
import jax.numpy as jnp
from jax import lax
from jax.experimental import pallas as pl
from jax.experimental.pallas import tpu as pltpu

F32 = jnp.float32
BF16 = jnp.bfloat16

EPS = 1e-6
CHUNK = 128
D = 1024
SSD_HEADS = 16
SSD_HEADDIM = 64
SSD_GROUPS = 2
SSD_STATE = 128
SSD_CONV = 4
SSD_XBC = D + 2 * SSD_GROUPS * SSD_STATE
LRU_BLOCKS = 4
LRU_BLOCK = D // LRU_BLOCKS
LRU_CONV = 4
LRU_C = 8.0
RET_HEADS = 4
RET_DK = 128
RET_DV = 256
ROPE_BASE = 10000.0
FFN_DIM = 3 * D
FFN_CONV = 3
LANES = 128
SUBLANES = 8
TAIL = SUBLANES
DT_SPLITS = 3
VMEM_LIMIT = 56 * 1024 * 1024

TS_MIX = 512
TS_FFN = 512
FFN_BLK = 1024


def _rmsnorm(x, g):
    return x * lax.rsqrt(jnp.mean(x * x, axis=-1, keepdims=True) + EPS) * g


def _mm(a, b):
    return jnp.dot(a.astype(BF16), b.astype(BF16), preferred_element_type=F32)


def _mm_tn(a, b):
    return lax.dot_general(a.astype(BF16), b.astype(BF16), (((0,), (0,)), ((), ())),
                           preferred_element_type=F32)


def _mm_nt(a, b):
    return lax.dot_general(a.astype(BF16), b.astype(BF16), (((1,), (1,)), ((), ())),
                           preferred_element_type=F32)


def _sigmoid(x):
    return jax.nn.sigmoid(x)


def _silu(x):
    return x * jax.nn.sigmoid(x)


def _gelu_tanh(x):
    c = math.sqrt(2.0 / math.pi)
    return 0.5 * x * (1.0 + jnp.tanh(c * (x + 0.044715 * (x * x * x))))


def _softplus(x):
    return jnp.maximum(x, 0.0) + jnp.log1p(jnp.exp(-jnp.abs(x)))


def _split3(v):
    s1 = v.astype(BF16).astype(F32)
    r1 = v - s1
    s2 = r1.astype(BF16).astype(F32)
    s3 = (r1 - s2).astype(BF16).astype(F32)
    return s1, s2, s3


def _conv_from_ext(ext_ref, w_ref, b_ref, taps, rows):
    y = ext_ref[TAIL:TAIL + rows, :] * w_ref[taps - 1:taps, :] + b_ref[...]
    for j in range(1, taps):
        y = y + ext_ref[TAIL - j:TAIL - j + rows, :] * w_ref[taps - 1 - j:taps - j, :]
    return y


def _const_spec(shape):
    nd = len(shape)
    return pl.BlockSpec(shape, lambda b, s: (0,) * nd, pipeline_mode=pl.Buffered(1))


def _tile_spec(ts, width):
    return pl.BlockSpec((None, ts, width), lambda b, s: (b, s, 0))


def _params():
    return pltpu.CompilerParams(dimension_semantics=("arbitrary", "arbitrary"),
                                vmem_limit_bytes=VMEM_LIMIT)


def _ffn_kernel(x_ref, g_ref, w_in_ref, cw_ref, cb_ref, w_out_ref, gf_ref, o_ref,
                ext_ref, tail_ref, *, final_norm):
    ts = x_ref.shape[0]
    s = pl.program_id(1)

    @pl.when(s == 0)
    def _():
        tail_ref[...] = jnp.zeros_like(tail_ref)

    x = x_ref[...]
    h = _rmsnorm(x, g_ref[...]).astype(BF16)
    acc = x
    for j in range(FFN_DIM // FFN_BLK):
        lo = j * FFN_BLK
        a = jnp.dot(h, w_in_ref[:, lo:lo + FFN_BLK], preferred_element_type=F32)
        u = jnp.dot(h, w_in_ref[:, FFN_DIM + lo:FFN_DIM + lo + FFN_BLK], preferred_element_type=F32)
        ext_ref[0:TAIL, :] = tail_ref[:, lo:lo + FFN_BLK]
        ext_ref[TAIL:TAIL + ts, :] = a
        tail_ref[:, lo:lo + FFN_BLK] = a[ts - TAIL:ts, :]
        ac = _conv_from_ext(ext_ref, cw_ref.at[:, lo:lo + FFN_BLK], cb_ref.at[:, lo:lo + FFN_BLK],
                            FFN_CONV, ts)
        act = (_gelu_tanh(ac) * u).astype(BF16)
        acc = acc + jnp.dot(act, w_out_ref[lo:lo + FFN_BLK, :], preferred_element_type=F32)
    if final_norm:
        acc = _rmsnorm(acc, gf_ref[...])
    o_ref[...] = acc


def _ffn_call(x, g, w_in, cw, cb, w_out, gf, final_norm):
    bsz, seq, _ = x.shape
    ts = min(TS_FFN, seq)
    return pl.pallas_call(
        functools.partial(_ffn_kernel, final_norm=final_norm),
        grid=(bsz, seq // ts),
        in_specs=[_tile_spec(ts, D), _const_spec((1, D)), _const_spec((D, 2 * FFN_DIM)),
                  _const_spec((FFN_CONV, FFN_DIM)), _const_spec((1, FFN_DIM)),
                  _const_spec((FFN_DIM, D)), _const_spec((1, D))],
        out_specs=_tile_spec(ts, D),
        out_shape=jax.ShapeDtypeStruct(x.shape, F32),
        scratch_shapes=[pltpu.VMEM((TAIL + ts, FFN_BLK), F32), pltpu.VMEM((TAIL, FFN_DIM), F32)],
        compiler_params=_params(),
        name="ffn",
    )(x, g, w_in, cw, cb, w_out, gf)


def _lru_kernel(x_ref, m_ref, g_ref, wy_ref, wx_ref, wg_ref, cw_ref, cb_ref, wa_ref, ba_ref, wi_ref,
                bi_ref, lam_ref, wbr_ref, o_ref, ext_ref, carry_ref):
    ts = x_ref.shape[0]
    s = pl.program_id(1)

    @pl.when(s == 0)
    def _():
        ext_ref[0:TAIL, :] = jnp.zeros((TAIL, D), F32)
        carry_ref[...] = jnp.zeros_like(carry_ref)

    h = _rmsnorm(x_ref[...], g_ref[...]).astype(BF16)
    ext_ref[TAIL:TAIL + ts, :] = jnp.dot(h, wx_ref[...], preferred_element_type=F32)
    xb = _conv_from_ext(ext_ref, cw_ref, cb_ref, LRU_CONV, ts)
    ext_ref[0:TAIL, :] = ext_ref[ts:ts + TAIL, :]

    xb16 = xb.astype(BF16)
    r_parts, i_parts = [], []
    for k in range(LRU_BLOCKS):
        blk = xb16[:, k * LRU_BLOCK:(k + 1) * LRU_BLOCK]
        r_parts.append(jnp.dot(blk, wa_ref[k], preferred_element_type=F32))
        i_parts.append(jnp.dot(blk, wi_ref[k], preferred_element_type=F32))
    r = _sigmoid(jnp.concatenate(r_parts, axis=1) + ba_ref[...])
    gi = _sigmoid(jnp.concatenate(i_parts, axis=1) + bi_ref[...])
    log_a = r * ((-LRU_C) * _softplus(-lam_ref[...]))
    a = jnp.exp(log_a)
    u = jnp.sqrt(-jnp.tanh(log_a) * (1.0 + a * a)) * (gi * xb)

    row = lax.broadcasted_iota(jnp.int32, (ts, D), 0)
    for d in (1, 2, 4):
        a_sh = jnp.where(row >= d, pltpu.roll(a, d, axis=0), 1.0)
        u_sh = jnp.where(row >= d, pltpu.roll(u, d, axis=0), 0.0)
        u = a * u_sh + u
        a = a * a_sh
    hk = carry_ref[...]
    hs = []
    for k in range(ts // SUBLANES):
        hk = a[k * SUBLANES:(k + 1) * SUBLANES, :] * hk + u[k * SUBLANES:(k + 1) * SUBLANES, :]
        hs.append(hk)
    carry_ref[...] = jnp.broadcast_to(hk[SUBLANES - 1:SUBLANES, :], (SUBLANES, D))
    hseq = jnp.concatenate(hs, axis=0)

    yg = jnp.dot(h, wy_ref[...], preferred_element_type=F32)
    y = (hseq * _gelu_tanh(yg)).astype(BF16)
    br = jnp.dot(y, wbr_ref[...], preferred_element_type=F32)
    gate = jnp.dot(h, wg_ref[...], preferred_element_type=F32)
    o_ref[...] = m_ref[...] + _sigmoid(gate) * br


def _lru_call(x, merged, g, wy, wx, wg, cw, cb, wa, ba, wi, bi, lam, wbr):
    bsz, seq, _ = x.shape
    ts = min(TS_MIX, seq)
    return pl.pallas_call(
        _lru_kernel,
        grid=(bsz, seq // ts),
        in_specs=[_tile_spec(ts, D), _tile_spec(ts, D), _const_spec((1, D)), _const_spec((D, D)),
                  _const_spec((D, D)), _const_spec((D, D)),
                  _const_spec((LRU_CONV, D)), _const_spec((1, D)),
                  _const_spec((LRU_BLOCKS, LRU_BLOCK, LRU_BLOCK)), _const_spec((1, D)),
                  _const_spec((LRU_BLOCKS, LRU_BLOCK, LRU_BLOCK)), _const_spec((1, D)),
                  _const_spec((1, D)), _const_spec((D, D))],
        out_specs=_tile_spec(ts, D),
        out_shape=jax.ShapeDtypeStruct(x.shape, F32),
        scratch_shapes=[pltpu.VMEM((TAIL + ts, D), F32), pltpu.VMEM((SUBLANES, D), F32)],
        compiler_params=_params(),
        name="lru",
    )(x, merged, g, wy, wx, wg, cw, cb, wa, ba, wi, bi, lam, wbr)


def _ret_kernel(x_ref, m_ref, g_ref, w_ref, wg_ref, inv_ref, wbr_ref, wo_ref, o_ref, state_ref):
    ts = x_ref.shape[0]
    s = pl.program_id(1)

    @pl.when(s == 0)
    def _():
        state_ref[...] = jnp.zeros_like(state_ref)

    x = x_ref[...]
    h = _rmsnorm(x, g_ref[...]).astype(BF16)
    qk_w = RET_HEADS * RET_DK
    q = jnp.dot(h, w_ref[:, 0:qk_w], preferred_element_type=F32)
    k = jnp.dot(h, w_ref[:, qk_w:2 * qk_w], preferred_element_type=F32)
    v = jnp.dot(h, w_ref[:, 2 * qk_w:2 * qk_w + D], preferred_element_type=F32).astype(BF16)

    pos = (s * ts + lax.broadcasted_iota(jnp.int32, (ts, LANES), 0)).astype(F32)
    ang = pos * inv_ref[...]
    lane = lax.broadcasted_iota(jnp.int32, (ts, LANES), 1)
    even = (lane & 1) == 0
    cos_t = jnp.cos(ang)
    sin_t = jnp.where(even, -jnp.sin(ang), jnp.sin(ang))

    def rotary(t):
        swapped = jnp.where(even, pltpu.roll(t, LANES - 1, axis=1), pltpu.roll(t, 1, axis=1))
        return t * cos_t + swapped * sin_t

    qr = [rotary(q[:, hd * RET_DK:(hd + 1) * RET_DK]) for hd in range(RET_HEADS)]
    kr = [rotary(k[:, hd * RET_DK:(hd + 1) * RET_DK]) * (RET_DK ** -0.5) for hd in range(RET_HEADS)]

    li = lax.broadcasted_iota(jnp.int32, (CHUNK, CHUNK), 0)
    si = lax.broadcasted_iota(jnp.int32, (CHUNK, CHUNK), 1)
    diff = (li - si).astype(F32)
    idx = li.astype(F32)
    log_gamma = [math.log(1.0 - 2.0 ** (-5.0 - hd)) for hd in range(RET_HEADS)]
    dmask = [jnp.where(diff >= 0.0, jnp.exp(jnp.maximum(diff, 0.0) * lg), 0.0) for lg in log_gamma]
    q_decay = [jnp.exp((idx + 1.0) * lg) for lg in log_gamma]
    k_decay = [jnp.exp((CHUNK - 1.0 - idx) * lg) for lg in log_gamma]
    chunk_decay = [math.exp(CHUNK * lg) for lg in log_gamma]

    head_rows = []
    for hd in range(RET_HEADS):
        st = state_ref[hd]
        outs = []
        for c in range(ts // CHUNK):
            rows = slice(c * CHUNK, (c + 1) * CHUNK)
            qh = qr[hd][rows, :]
            kh = kr[hd][rows, :]
            vh = v[rows, hd * RET_DV:(hd + 1) * RET_DV]
            scores = _mm_nt(qh, kh) * dmask[hd]
            lhs = jnp.concatenate([scores.astype(BF16), (qh * q_decay[hd]).astype(BF16)], axis=1)
            rhs = jnp.concatenate([vh, st.astype(BF16)], axis=0)
            out = jnp.dot(lhs, rhs, preferred_element_type=F32)
            st = st * chunk_decay[hd] + _mm_tn(kh * k_decay[hd], vh)
            outs.append(out * lax.rsqrt(jnp.mean(out * out, axis=-1, keepdims=True) + EPS))
        state_ref[hd] = st
        head_rows.append(jnp.concatenate(outs, axis=0))
    yret = jnp.concatenate(head_rows, axis=1)

    gr = jnp.dot(h, w_ref[:, 2 * qk_w + D:2 * qk_w + 2 * D], preferred_element_type=F32)
    y = (yret * _silu(gr)).astype(BF16)
    br = jnp.dot(y, wbr_ref[...], preferred_element_type=F32)
    gate = jnp.dot(h, wg_ref[...], preferred_element_type=F32)
    merged = (m_ref[...] + _sigmoid(gate) * br).astype(BF16)
    o_ref[...] = x + jnp.dot(merged, wo_ref[...], preferred_element_type=F32)


def _ret_call(x, merged, g, w, wg, inv, wbr, wo):
    bsz, seq, _ = x.shape
    ts = min(TS_MIX, seq)
    return pl.pallas_call(
        _ret_kernel,
        grid=(bsz, seq // ts),
        in_specs=[_tile_spec(ts, D), _tile_spec(ts, D), _const_spec((1, D)),
                  _const_spec((D, 2 * RET_HEADS * RET_DK + 2 * D)), _const_spec((D, D)),
                  _const_spec((1, LANES)), _const_spec((D, D)), _const_spec((D, D))],
        out_specs=_tile_spec(ts, D),
        out_shape=jax.ShapeDtypeStruct(x.shape, F32),
        scratch_shapes=[pltpu.VMEM((RET_HEADS, RET_DK, RET_DV), F32)],
        compiler_params=_params(),
        name="ret",
    )(x, merged, g, w, wg, inv, wbr, wo)


def _ssd_kernel(x_ref, g_ref, w_ref, wdt_ref, wg_ref, cw_ref, cb_ref, dtb_ref, alog_ref, dskip_ref,
                nw_ref, e1_ref, e2_ref, tri_ref, wbr_ref, o_ref,
                ext_ref, xbc_ref, z_ref, dt_ref, state_ref):
    ts = x_ref.shape[0]
    s = pl.program_id(1)

    @pl.when(s == 0)
    def _():
        ext_ref[0:TAIL, :] = jnp.zeros((TAIL, SSD_XBC), F32)
        state_ref[...] = jnp.zeros_like(state_ref)

    h = _rmsnorm(x_ref[...], g_ref[...]).astype(BF16)
    z_ref[...] = jnp.dot(h, w_ref[:, 0:D], preferred_element_type=F32)
    ext_ref[TAIL:TAIL + ts, :] = jnp.dot(h, w_ref[:, D:D + SSD_XBC], preferred_element_type=F32)
    xbc_ref[...] = _silu(_conv_from_ext(ext_ref, cw_ref, cb_ref, SSD_CONV, ts))
    ext_ref[0:TAIL, :] = ext_ref[ts:ts + TAIL, :]
    dt_ref[...] = _softplus(jnp.dot(h, wdt_ref[...], preferred_element_type=F32) + dtb_ref[...])

    a_head = -jnp.exp(alog_ref[...])
    lane = lax.broadcasted_iota(jnp.int32, (CHUNK, LANES), 1)
    li = lax.broadcasted_iota(jnp.int32, (CHUNK, CHUNK), 0)
    si = lax.broadcasted_iota(jnp.int32, (CHUNK, CHUNK), 1)
    causal = li >= si
    lo_half = lane < SSD_HEADDIM
    gw = SSD_STATE
    hp = D // SSD_GROUPS

    def pack3(v):
        s1, s2, s3 = _split3(v)
        return jnp.where(lane < SSD_HEADS, s1, jnp.where(lane < 2 * SSD_HEADS, s2, s3)).astype(BF16)

    states = [state_ref[:, g * hp:(g + 1) * hp] for g in range(SSD_GROUPS)]
    y_rows = []
    for c in range(ts // CHUNK):
        rows = slice(c * CHUNK, (c + 1) * CHUNK)
        dt = dt_ref[rows, :]
        a = dt * a_head
        s1, s2, s3 = _split3(a)
        tri = tri_ref[...]
        acs = (jnp.dot(tri, s1.astype(BF16), preferred_element_type=F32)
               + jnp.dot(tri, s2.astype(BF16), preferred_element_type=F32)
               + jnp.dot(tri, s3.astype(BF16), preferred_element_type=F32))
        ea = jnp.exp(acs)
        wst = jnp.exp(acs[CHUNK - 1:CHUNK, :] - acs) * dt
        exp1 = jnp.dot(jnp.concatenate([pack3(ea), pack3(wst)], axis=0), e1_ref[...],
                       preferred_element_type=F32)
        ea_x = exp1[0:CHUNK, :]
        wst_x = exp1[CHUNK:2 * CHUNK, :]
        acs_x = jnp.dot(pack3(acs), e2_ref[...], preferred_element_type=F32)
        acs_t = acs.T
        dt_t = dt.T

        xs = xbc_ref[rows, 0:D]
        bm = xbc_ref[rows, D:D + SSD_GROUPS * gw]
        cm = xbc_ref[rows, D + SSD_GROUPS * gw:D + 2 * SSD_GROUPS * gw]
        xs16 = xs.astype(BF16)
        xw16 = (xs * wst_x).astype(BF16)

        y_parts = []
        for g in range(SSD_GROUPS):
            bg = bm[:, g * gw:(g + 1) * gw]
            cg = cm[:, g * gw:(g + 1) * gw]
            cb = _mm_nt(cg, bg)
            y_off = _mm(cg, states[g]) * ea_x[:, g * hp:(g + 1) * hp]
            dec_last = ea_x[CHUNK - 1:CHUNK, g * hp:(g + 1) * hp]
            states[g] = states[g] * dec_last + _mm_tn(bg, xw16[:, g * hp:(g + 1) * hp])
            heads_per_group = SSD_HEADS // SSD_GROUPS
            diag_parts = []
            for pr in range(heads_per_group // 2):
                ms = []
                for hh in (2 * pr, 2 * pr + 1):
                    hd = g * heads_per_group + hh
                    seg = acs_x[:, hd * LANES:(hd + 1) * LANES] - acs_t[hd:hd + 1, :]
                    dec = jnp.exp(jnp.where(causal, seg, -jnp.inf))
                    ms.append((cb * dec * dt_t[hd:hd + 1, :]).astype(BF16))
                col = g * hp + pr * LANES
                xp = xs16[:, col:col + LANES]
                zero = jnp.zeros_like(xp)
                rhs = jnp.concatenate([jnp.where(lo_half, xp, zero), jnp.where(lo_half, zero, xp)], axis=0)
                diag_parts.append(jnp.dot(jnp.concatenate(ms, axis=1), rhs, preferred_element_type=F32))
            y_parts.append(jnp.concatenate(diag_parts, axis=1) + y_off)
        y = jnp.concatenate(y_parts, axis=1) + xs * dskip_ref[...]
        y = y * _silu(z_ref[rows, :])
        y_rows.append(_rmsnorm(y, nw_ref[...]).astype(BF16))
    for g in range(SSD_GROUPS):
        state_ref[:, g * hp:(g + 1) * hp] = states[g]

    br = jnp.dot(jnp.concatenate(y_rows, axis=0), wbr_ref[...], preferred_element_type=F32)
    gate = jnp.dot(h, wg_ref[...], preferred_element_type=F32)
    o_ref[...] = _sigmoid(gate) * br


def _ssd_call(x, g, w, wdt, wg, cw, cb, dtb, alog, dskip, nw, e1, e2, tri, wbr):
    bsz, seq, _ = x.shape
    ts = min(TS_MIX, seq)
    return pl.pallas_call(
        _ssd_kernel,
        grid=(bsz, seq // ts),
        in_specs=[_tile_spec(ts, D), _const_spec((1, D)), _const_spec((D, D + SSD_XBC)),
                  _const_spec((D, LANES)), _const_spec((D, D)),
                  _const_spec((SSD_CONV, SSD_XBC)), _const_spec((1, SSD_XBC)),
                  _const_spec((1, LANES)), _const_spec((1, LANES)), _const_spec((1, D)),
                  _const_spec((1, D)), _const_spec((LANES, D)), _const_spec((LANES, SSD_HEADS * LANES)),
                  _const_spec((CHUNK, CHUNK)), _const_spec((D, D))],
        out_specs=_tile_spec(ts, D),
        out_shape=jax.ShapeDtypeStruct(x.shape, F32),
        scratch_shapes=[pltpu.VMEM((TAIL + ts, SSD_XBC), F32), pltpu.VMEM((ts, SSD_XBC), F32),
                        pltpu.VMEM((ts, D), F32), pltpu.VMEM((ts, LANES), F32),
                        pltpu.VMEM((SSD_STATE, D), F32)],
        compiler_params=_params(),
        name="ssd",
    )(x, g, w, wdt, wg, cw, cb, dtb, alog, dskip, nw, e1, e2, tri, wbr)


def _head_expand_matrix(width_per_head):
    r = jnp.arange(LANES)[:, None]
    c = jnp.arange(SSD_HEADS * width_per_head)[None, :]
    hit = (r < DT_SPLITS * SSD_HEADS) & ((c // width_per_head) == (r % SSD_HEADS))
    return hit.astype(BF16)


def _rep_heads(v):
    rep = jnp.tile(v.astype(F32), DT_SPLITS)
    return jnp.pad(rep, (0, LANES - rep.shape[0]))[None, :]


def kernel(x, norm_mix, w_in, ssd_conv_w, ssd_conv_b, ssd_dt_bias, ssd_a_log, ssd_d, ssd_norm,
           lru_conv_w, lru_conv_b, lru_w_a, lru_b_a, lru_w_i, lru_b_i, lru_lambda,
           w_branch, w_o, norm_ffn, ffn_w_in, ffn_conv_w, ffn_conv_b, ffn_w_out, norm_final):
    depth = w_in.shape[0]
    o_z, o_xbc, o_dt = 0, D, D + SSD_XBC
    o_ly = o_dt + SSD_HEADS
    o_lx, o_q = o_ly + D, o_ly + 2 * D
    o_k = o_q + RET_HEADS * RET_DK
    o_v = o_k + RET_HEADS * RET_DK
    o_rg, o_gate = o_v + D, o_v + 2 * D

    e1 = _head_expand_matrix(SSD_HEADDIM)
    e2 = _head_expand_matrix(LANES)
    tri = (jnp.arange(CHUNK)[:, None] >= jnp.arange(CHUNK)[None, :]).astype(BF16)
    inv = 1.0 / (ROPE_BASE ** jnp.linspace(0.0, 1.0, RET_DK // 2, dtype=F32))
    inv_row = jnp.repeat(inv, 2)[None, :]

    row = lambda v: v.astype(F32)[None, :]
    for l in range(depth):
        cols = lambda lo, hi: w_in[l, :, lo:hi].astype(BF16)
        wdt = jnp.pad(jnp.tile(cols(o_dt, o_ly), (1, DT_SPLITS)), ((0, 0), (0, LANES - DT_SPLITS * SSD_HEADS)))
        gate_w = [cols(o_gate + i * D, o_gate + (i + 1) * D) for i in range(3)]
        wbr = [w_branch[l, i].astype(BF16) for i in range(3)]

        merged = _ssd_call(x, row(norm_mix[l]), cols(o_z, o_dt), wdt, gate_w[0], ssd_conv_w[l],
                           row(ssd_conv_b[l]), _rep_heads(ssd_dt_bias[l]), _rep_heads(ssd_a_log[l]),
                           row(jnp.repeat(ssd_d[l], SSD_HEADDIM)), row(ssd_norm[l]), e1, e2, tri, wbr[0])
        merged = _lru_call(x, merged, row(norm_mix[l]), cols(o_ly, o_lx), cols(o_lx, o_q), gate_w[1],
                           lru_conv_w[l], row(lru_conv_b[l]), lru_w_a[l].astype(BF16), row(lru_b_a[l]),
                           lru_w_i[l].astype(BF16), row(lru_b_i[l]), row(lru_lambda[l]), wbr[1])
        x = _ret_call(x, merged, row(norm_mix[l]), cols(o_q, o_gate), gate_w[2], inv_row, wbr[2],
                      w_o[l].astype(BF16))
        x = _ffn_call(x, row(norm_ffn[l]), ffn_w_in[l].astype(BF16), ffn_conv_w[l], row(ffn_conv_b[l]),
                      ffn_w_out[l].astype(BF16), row(norm_final), final_norm=(l == depth - 1))
    return x
```

```python
import functools
import math

import jax
import jax.numpy as jnp
from jax import lax
from jax.experimental import pallas as pl
from jax.experimental.pallas import tpu as pltpu

F32 = jnp.float32
BF16 = jnp.bfloat16

EPS = 1e-6
CHUNK = 128
D = 1024
SSD_HEADS = 16
SSD_HEADDIM = 64
SSD_GROUPS = 2
SSD_STATE = 128
SSD_CONV = 4
SSD_BC = 2 * SSD_GROUPS * SSD_STATE
SSD_XBC = D + SSD_BC
LRU_BLOCKS = 4
LRU_BLOCK = D // LRU_BLOCKS
LRU_CONV = 4
LRU_C = 8.0
RET_HEADS = 4
RET_DK = 128
RET_DV = 256
RET_QK = RET_HEADS * RET_DK
ROPE_BASE = 10000.0
FFN_DIM = 3 * D
FFN_CONV = 3
N_BRANCHES = 3
LANES = 128
SUBLANES = 8
TAIL = SUBLANES
DT_SPLITS = 3
VMEM_LIMIT = 58 * 1024 * 1024

TS_MIX = 256
TS_FFN = 512
FFN_BLK = 1024

P_Z = 0
P_XBC = P_Z + D
P_DT = P_XBC + SSD_XBC
P_LY = P_DT + LANES
P_LX = P_LY + D
P_Q = P_LX + D
P_K = P_Q + RET_QK
P_V = P_K + RET_QK
P_RG = P_V + D
P_GATE = P_RG + D
P_WIDTH = P_GATE + N_BRANCHES * D


def _rmsnorm(x, g):
    return x * lax.rsqrt(jnp.mean(x * x, axis=-1, keepdims=True) + EPS) * g


def _mm(a, b):
    return jnp.dot(a.astype(BF16), b.astype(BF16), preferred_element_type=F32)


def _mm_tn(a, b):
    return lax.dot_general(a.astype(BF16), b.astype(BF16), (((0,), (0,)), ((), ())),
                           preferred_element_type=F32)


def _mm_nt(a, b):
    return lax.dot_general(a.astype(BF16), b.astype(BF16), (((1,), (1,)), ((), ())),
                           preferred_element_type=F32)


def _sigmoid(x):
    return jax.nn.sigmoid(x)


def _silu(x):
    return x * jax.nn.sigmoid(x)


def _gelu_tanh(x):
    c = math.sqrt(2.0 / math.pi)
    return 0.5 * x * (1.0 + jnp.tanh(c * (x + 0.044715 * (x * x * x))))


def _softplus(x):
    return jnp.maximum(x, 0.0) + jnp.log1p(jnp.exp(-jnp.abs(x)))


def _split3(v):
    s1 = v.astype(BF16).astype(F32)
    r1 = v - s1
    s2 = r1.astype(BF16).astype(F32)
    s3 = (r1 - s2).astype(BF16).astype(F32)
    return s1, s2, s3


def _conv_from_ext(ext_ref, w_ref, b_ref, taps, rows):
    y = ext_ref[TAIL:TAIL + rows, :] * w_ref[taps - 1:taps, :] + b_ref[...]
    for j in range(1, taps):
        y = y + ext_ref[TAIL - j:TAIL - j + rows, :] * w_ref[taps - 1 - j:taps - j, :]
    return y


def _const_spec(shape):
    nd = len(shape)
    return pl.BlockSpec(shape, lambda b, s: (0,) * nd, pipeline_mode=pl.Buffered(1))


def _tile_spec(ts, width):
    return pl.BlockSpec((None, ts, width), lambda b, s: (b, s, 0))


def _params():
    return pltpu.CompilerParams(dimension_semantics=("arbitrary", "arbitrary"),
                                vmem_limit_bytes=VMEM_LIMIT)


def _ffn_kernel(x_ref, g_ref, w_in_ref, cw_ref, cb_ref, w_out_ref, gf_ref, o_ref,
                ext_ref, tail_ref, *, final_norm):
    ts = x_ref.shape[0]
    s = pl.program_id(1)

    @pl.when(s == 0)
    def _():
        tail_ref[...] = jnp.zeros_like(tail_ref)

    x = x_ref[...]
    h = _rmsnorm(x, g_ref[...]).astype(BF16)
    acc = x
    for j in range(FFN_DIM // FFN_BLK):
        lo = j * FFN_BLK
        a = jnp.dot(h, w_in_ref[:, lo:lo + FFN_BLK], preferred_element_type=F32)
        u = jnp.dot(h, w_in_ref[:, FFN_DIM + lo:FFN_DIM + lo + FFN_BLK], preferred_element_type=F32)
        ext_ref[0:TAIL, :] = tail_ref[:, lo:lo + FFN_BLK]
        ext_ref[TAIL:TAIL + ts, :] = a
        tail_ref[:, lo:lo + FFN_BLK] = a[ts - TAIL:ts, :]
        ac = _conv_from_ext(ext_ref, cw_ref.at[:, lo:lo + FFN_BLK], cb_ref.at[:, lo:lo + FFN_BLK],
                            FFN_CONV, ts)
        act = (_gelu_tanh(ac) * u).astype(BF16)
        acc = acc + jnp.dot(act, w_out_ref[lo:lo + FFN_BLK, :], preferred_element_type=F32)
    if final_norm:
        acc = _rmsnorm(acc, gf_ref[...])
    o_ref[...] = acc


def _ffn_call(x, g, w_in, cw, cb, w_out, gf, final_norm):
    bsz, seq, _ = x.shape
    ts = min(TS_FFN, seq)
    return pl.pallas_call(
        functools.partial(_ffn_kernel, final_norm=final_norm),
        grid=(bsz, seq // ts),
        in_specs=[_tile_spec(ts, D), _const_spec((1, D)), _const_spec((D, 2 * FFN_DIM)),
                  _const_spec((FFN_CONV, FFN_DIM)), _const_spec((1, FFN_DIM)),
                  _const_spec((FFN_DIM, D)), _const_spec((1, D))],
        out_specs=_tile_spec(ts, D),
        out_shape=jax.ShapeDtypeStruct(x.shape, F32),
        scratch_shapes=[pltpu.VMEM((TAIL + ts, FFN_BLK), F32), pltpu.VMEM((TAIL, FFN_DIM), F32)],
        compiler_params=_params(),
        name="ffn",
    )(x, g, w_in, cw, cb, w_out, gf)


def _mixer_kernel(x_ref, g_ref, w_ref,
                  scw_ref, scb_ref, dtb_ref, alog_ref, dskip_ref, nw_ref, e1_ref, e2_ref, tri_ref,
                  lcw_ref, lcb_ref, wa_ref, ba_ref, wi_ref, bi_ref, lam_ref,
                  inv_ref, wbr_ref, wo_ref, o_ref,
                  sext_ref, sstate_ref, lext_ref, carry_ref, rstate_ref):
    ts = x_ref.shape[0]
    s = pl.program_id(1)
    n_chunks = ts // CHUNK

    @pl.when(s == 0)
    def _():
        sext_ref[0:TAIL, :] = jnp.zeros((TAIL, SSD_XBC), F32)
        sstate_ref[...] = jnp.zeros_like(sstate_ref)
        lext_ref[0:TAIL, :] = jnp.zeros((TAIL, D), F32)
        carry_ref[...] = jnp.zeros_like(carry_ref)
        rstate_ref[...] = jnp.zeros_like(rstate_ref)

    x = x_ref[...]
    h = _rmsnorm(x, g_ref[...]).astype(BF16)

    def proj(lo, width):
        return jnp.dot(h, w_ref[:, lo:lo + width], preferred_element_type=F32)

    lext_ref[TAIL:TAIL + ts, :] = proj(P_LX, D)
    xb = _conv_from_ext(lext_ref, lcw_ref, lcb_ref, LRU_CONV, ts)
    lext_ref[0:TAIL, :] = lext_ref[ts:ts + TAIL, :]
    sext_ref[TAIL:TAIL + ts, :] = proj(P_XBC, SSD_XBC)

    xb16 = xb.astype(BF16)
    r_parts, i_parts = [], []
    for k in range(LRU_BLOCKS):
        blk = xb16[:, k * LRU_BLOCK:(k + 1) * LRU_BLOCK]
        r_parts.append(jnp.dot(blk, wa_ref[k], preferred_element_type=F32))
        i_parts.append(jnp.dot(blk, wi_ref[k], preferred_element_type=F32))
    z = proj(P_Z, D)
    dt_all = _softplus(proj(P_DT, LANES) + dtb_ref[...])
    r = _sigmoid(jnp.concatenate(r_parts, axis=1) + ba_ref[...])
    gi = _sigmoid(jnp.concatenate(i_parts, axis=1) + bi_ref[...])
    log_a = r * ((-LRU_C) * _softplus(-lam_ref[...]))
    a = jnp.exp(log_a)
    u = jnp.sqrt(-jnp.tanh(log_a) * (1.0 + a * a)) * (gi * xb)
    q = proj(P_Q, RET_QK)
    k = proj(P_K, RET_QK)

    row = lax.broadcasted_iota(jnp.int32, (ts, D), 0)
    for d in (1, 2, 4):
        a_sh = jnp.where(row >= d, pltpu.roll(a, d, axis=0), 1.0)
        u_sh = jnp.where(row >= d, pltpu.roll(u, d, axis=0), 0.0)
        u = a * u_sh + u
        a = a * a_sh
    v = proj(P_V, D).astype(BF16)
    hk = carry_ref[...]
    hs = []
    for kb in range(ts // SUBLANES):
        hk = a[kb * SUBLANES:(kb + 1) * SUBLANES, :] * hk + u[kb * SUBLANES:(kb + 1) * SUBLANES, :]
        hs.append(hk)
    carry_ref[...] = jnp.broadcast_to(hk[SUBLANES - 1:SUBLANES, :], (SUBLANES, D))
    hseq = jnp.concatenate(hs, axis=0)

    yg = proj(P_LY, D)
    xbc = _silu(_conv_from_ext(sext_ref, scw_ref, scb_ref, SSD_CONV, ts))
    sext_ref[0:TAIL, :] = sext_ref[ts:ts + TAIL, :]
    gate_pre = [proj(P_GATE + i * D, D) for i in range(N_BRANCHES)]
    y_lru = (hseq * _gelu_tanh(yg)).astype(BF16)

    pos = (s * ts + lax.broadcasted_iota(jnp.int32, (ts, LANES), 0)).astype(F32)
    ang = pos * inv_ref[...]
    lane_t = lax.broadcasted_iota(jnp.int32, (ts, LANES), 1)
    even = (lane_t & 1) == 0
    cos_t = jnp.cos(ang)
    sin_t = jnp.where(even, -jnp.sin(ang), jnp.sin(ang))

    def rotary(t):
        swapped = jnp.where(even, pltpu.roll(t, LANES - 1, axis=1), pltpu.roll(t, 1, axis=1))
        return t * cos_t + swapped * sin_t

    qr = [rotary(q[:, hd * RET_DK:(hd + 1) * RET_DK]) for hd in range(RET_HEADS)]
    kr = [rotary(k[:, hd * RET_DK:(hd + 1) * RET_DK]) * (RET_DK ** -0.5) for hd in range(RET_HEADS)]
    br_lru = jnp.dot(y_lru, wbr_ref[1], preferred_element_type=F32)
    gr = proj(P_RG, D)

    li = lax.broadcasted_iota(jnp.int32, (CHUNK, CHUNK), 0)
    si = lax.broadcasted_iota(jnp.int32, (CHUNK, CHUNK), 1)
    causal = li >= si
    diff = (li - si).astype(F32)
    idx = li.astype(F32)
    log_gamma = [math.log(1.0 - 2.0 ** (-5.0 - hd)) for hd in range(RET_HEADS)]
    dmask = [jnp.where(diff >= 0.0, jnp.exp(jnp.maximum(diff, 0.0) * lg), 0.0) for lg in log_gamma]
    q_decay = [jnp.exp((idx + 1.0) * lg) for lg in log_gamma]
    k_decay = [jnp.exp((CHUNK - 1.0 - idx) * lg) for lg in log_gamma]
    chunk_decay = [math.exp(CHUNK * lg) for lg in log_gamma]

    a_head = -jnp.exp(alog_ref[...])
    lane = lax.broadcasted_iota(jnp.int32, (CHUNK, LANES), 1)
    lo_half = lane < SSD_HEADDIM
    gw = SSD_STATE
    hp = D // SSD_GROUPS
    heads_per_group = SSD_HEADS // SSD_GROUPS

    def pack3(t):
        s1, s2, s3 = _split3(t)
        return jnp.where(lane < SSD_HEADS, s1, jnp.where(lane < 2 * SSD_HEADS, s2, s3)).astype(BF16)

    sstates = [sstate_ref[:, g * hp:(g + 1) * hp] for g in range(SSD_GROUPS)]
    rstates = [rstate_ref[hd] for hd in range(RET_HEADS)]
    y_ssd_rows, y_ret_rows = [], []
    for c in range(n_chunks):
        rows = slice(c * CHUNK, (c + 1) * CHUNK)
        dt = dt_all[rows, :]
        da = dt * a_head
        s1, s2, s3 = _split3(da)
        tri = tri_ref[...]
        acs = (jnp.dot(tri, s1.astype(BF16), preferred_element_type=F32)
               + jnp.dot(tri, s2.astype(BF16), preferred_element_type=F32)
               + jnp.dot(tri, s3.astype(BF16), preferred_element_type=F32))
        ea = jnp.exp(acs)
        wst = jnp.exp(acs[CHUNK - 1:CHUNK, :] - acs) * dt
        exp1 = jnp.dot(jnp.concatenate([pack3(ea), pack3(wst)], axis=0), e1_ref[...],
                       preferred_element_type=F32)
        ea_x = exp1[0:CHUNK, :]
        wst_x = exp1[CHUNK:2 * CHUNK, :]
        acs_x = jnp.dot(pack3(acs), e2_ref[...], preferred_element_type=F32)
        acs_t = acs.T
        dt_t = dt.T

        xs = xbc[rows, 0:D]
        bm = xbc[rows, D:D + SSD_GROUPS * gw]
        cm = xbc[rows, D + SSD_GROUPS * gw:D + 2 * SSD_GROUPS * gw]
        xs16 = xs.astype(BF16)
        xw16 = (xs * wst_x).astype(BF16)

        y_parts = []
        for g in range(SSD_GROUPS):
            bg = bm[:, g * gw:(g + 1) * gw]
            cg = cm[:, g * gw:(g + 1) * gw]
            cb = _mm_nt(cg, bg)
            y_off = _mm(cg, sstates[g]) * ea_x[:, g * hp:(g + 1) * hp]
            dec_last = ea_x[CHUNK - 1:CHUNK, g * hp:(g + 1) * hp]
            sstates[g] = sstates[g] * dec_last + _mm_tn(bg, xw16[:, g * hp:(g + 1) * hp])
            diag_parts = []
            for pr in range(heads_per_group // 2):
                ms = []
                for hh in (2 * pr, 2 * pr + 1):
                    hd = g * heads_per_group + hh
                    seg = acs_x[:, hd * LANES:(hd + 1) * LANES] - acs_t[hd:hd + 1, :]
                    dec = jnp.exp(jnp.where(causal, seg, -jnp.inf))
                    ms.append((cb * dec * dt_t[hd:hd + 1, :]).astype(BF16))
                col = g * hp + pr * LANES
                xp = xs16[:, col:col + LANES]
                zero = jnp.zeros_like(xp)
                rhs = jnp.concatenate([jnp.where(lo_half, xp, zero), jnp.where(lo_half, zero, xp)], axis=0)
                diag_parts.append(jnp.dot(jnp.concatenate(ms, axis=1), rhs, preferred_element_type=F32))
            y_parts.append(jnp.concatenate(diag_parts, axis=1) + y_off)
        y = jnp.concatenate(y_parts, axis=1) + xs * dskip_ref[...]
        y = y * _silu(z[rows, :])
        y_ssd_rows.append(_rmsnorm(y, nw_ref[...]).astype(BF16))

        outs = []
        for hd in range(RET_HEADS):
            qh = qr[hd][rows, :]
            kh = kr[hd][rows, :]
            vh = v[rows, hd * RET_DV:(hd + 1) * RET_DV]
            scores = _mm_nt(qh, kh) * dmask[hd]
            lhs = jnp.concatenate([scores.astype(BF16), (qh * q_decay[hd]).astype(BF16)], axis=1)
            rhs = jnp.concatenate([vh, rstates[hd].astype(BF16)], axis=0)
            out = jnp.dot(lhs, rhs, preferred_element_type=F32)
            rstates[hd] = rstates[hd] * chunk_decay[hd] + _mm_tn(kh * k_decay[hd], vh)
            outs.append(out * lax.rsqrt(jnp.mean(out * out, axis=-1, keepdims=True) + EPS))
        y_ret_rows.append((jnp.concatenate(outs, axis=1) * _silu(gr[rows, :])).astype(BF16))
    for g in range(SSD_GROUPS):
        sstate_ref[:, g * hp:(g + 1) * hp] = sstates[g]
    for hd in range(RET_HEADS):
        rstate_ref[hd] = rstates[hd]

    merged = _sigmoid(gate_pre[1]) * br_lru
    br_ssd = jnp.dot(jnp.concatenate(y_ssd_rows, axis=0), wbr_ref[0], preferred_element_type=F32)
    merged = merged + _sigmoid(gate_pre[0]) * br_ssd
    br_ret = jnp.dot(jnp.concatenate(y_ret_rows, axis=0), wbr_ref[2], preferred_element_type=F32)
    merged = merged + _sigmoid(gate_pre[2]) * br_ret
    o_ref[...] = x + jnp.dot(merged.astype(BF16), wo_ref[...], preferred_element_type=F32)


def _mixer_call(x, g, w, scw, scb, dtb, alog, dskip, nw, e1, e2, tri, lcw, lcb, wa, ba, wi, bi, lam,
                inv, wbr, wo):
    bsz, seq, _ = x.shape
    ts = min(TS_MIX, seq)
    blocks = (LRU_BLOCKS, LRU_BLOCK, LRU_BLOCK)
    return pl.pallas_call(
        _mixer_kernel,
        grid=(bsz, seq // ts),
        in_specs=[_tile_spec(ts, D), _const_spec((1, D)), _const_spec((D, P_WIDTH)),
                  _const_spec((SSD_CONV, SSD_XBC)), _const_spec((1, SSD_XBC)), _const_spec((1, LANES)),
                  _const_spec((1, LANES)), _const_spec((1, D)), _const_spec((1, D)),
                  _const_spec((LANES, D)), _const_spec((LANES, SSD_HEADS * LANES)),
                  _const_spec((CHUNK, CHUNK)),
                  _const_spec((LRU_CONV, D)), _const_spec((1, D)), _const_spec(blocks), _const_spec((1, D)),
                  _const_spec(blocks), _const_spec((1, D)), _const_spec((1, D)),
                  _const_spec((1, LANES)), _const_spec((N_BRANCHES, D, D)), _const_spec((D, D))],
        out_specs=_tile_spec(ts, D),
        out_shape=jax.ShapeDtypeStruct(x.shape, F32),
        scratch_shapes=[pltpu.VMEM((TAIL + ts, SSD_XBC), F32), pltpu.VMEM((SSD_STATE, D), F32),
                        pltpu.VMEM((TAIL + ts, D), F32), pltpu.VMEM((SUBLANES, D), F32),
                        pltpu.VMEM((RET_HEADS, RET_DK, RET_DV), F32)],
        compiler_params=_params(),
        name="mixer",
    )(x, g, w, scw, scb, dtb, alog, dskip, nw, e1, e2, tri, lcw, lcb, wa, ba, wi, bi, lam, inv, wbr, wo)


def _head_expand_matrix(width_per_head):
    r = jnp.arange(LANES)[:, None]
    c = jnp.arange(SSD_HEADS * width_per_head)[None, :]
    hit = (r < DT_SPLITS * SSD_HEADS) & ((c // width_per_head) == (r % SSD_HEADS))
    return hit.astype(BF16)


def _rep_heads(v):
    rep = jnp.tile(v.astype(F32), DT_SPLITS)
    return jnp.pad(rep, (0, LANES - rep.shape[0]))[None, :]


def _repack_in_proj(w):
    o_dt = D + SSD_XBC
    o_rest = o_dt + SSD_HEADS
    dt_cols = jnp.tile(w[:, o_dt:o_rest], (1, DT_SPLITS))
    pad = jnp.zeros((D, LANES - DT_SPLITS * SSD_HEADS), w.dtype)
    return jnp.concatenate([w[:, :o_dt], dt_cols, pad, w[:, o_rest:]], axis=1).astype(BF16)


def kernel(x, norm_mix, w_in, ssd_conv_w, ssd_conv_b, ssd_dt_bias, ssd_a_log, ssd_d, ssd_norm,
           lru_conv_w, lru_conv_b, lru_w_a, lru_b_a, lru_w_i, lru_b_i, lru_lambda,
           w_branch, w_o, norm_ffn, ffn_w_in, ffn_conv_w, ffn_conv_b, ffn_w_out, norm_final):
    depth = w_in.shape[0]
    e1 = _head_expand_matrix(SSD_HEADDIM)
    e2 = _head_expand_matrix(LANES)
    tri = (jnp.arange(CHUNK)[:, None] >= jnp.arange(CHUNK)[None, :]).astype(BF16)
    inv = 1.0 / (ROPE_BASE ** jnp.linspace(0.0, 1.0, RET_DK // 2, dtype=F32))
    inv_row = jnp.repeat(inv, 2)[None, :]

    row = lambda v: v.astype(F32)[None, :]
    for l in range(depth):
        x = _mixer_call(x, row(norm_mix[l]), _repack_in_proj(w_in[l]),
                        ssd_conv_w[l], row(ssd_conv_b[l]), _rep_heads(ssd_dt_bias[l]),
                        _rep_heads(ssd_a_log[l]), row(jnp.repeat(ssd_d[l], SSD_HEADDIM)), row(ssd_norm[l]),
                        e1, e2, tri,
                        lru_conv_w[l], row(lru_conv_b[l]), lru_w_a[l].astype(BF16), row(lru_b_a[l]),
                        lru_w_i[l].astype(BF16), row(lru_b_i[l]), row(lru_lambda[l]),
                        inv_row, w_branch[l].astype(BF16), w_o[l].astype(BF16))
        x = _ffn_call(x, row(norm_ffn[l]), ffn_w_in[l].astype(BF16), ffn_conv_w[l], row(ffn_conv_b[l]),
                      ffn_w_out[l].astype(BF16), row(norm_final), final_norm=(l == depth - 1))
    return x
```

```python
import functools
import math

import jax
import jax.numpy as jnp
from jax import lax
from jax.experimental import pallas as pl
from jax.experimental.pallas import tpu as pltpu

F32 = jnp.float32
BF16 = jnp.bfloat16

EPS = 1e-6
CHUNK = 128
D = 1024
SSD_HEADS = 16
SSD_HEADDIM = 64
SSD_GROUPS = 2
SSD_STATE = 128
SSD_CONV = 4
SSD_BC = 2 * SSD_GROUPS * SSD_STATE
SSD_XBC = D + SSD_BC
LRU_BLOCKS = 4
LRU_BLOCK = D // LRU_BLOCKS
LRU_CONV = 4
LRU_C = 8.0
RET_HEADS = 4
RET_DK = 128
RET_DV = 256
RET_QK = RET_HEADS * RET_DK
ROPE_BASE = 10000.0
FFN_DIM = 3 * D
FFN_CONV = 3
N_BRANCHES = 3
LANES = 128
SUBLANES = 8
TAIL = SUBLANES
DT_SPLITS = 3
VMEM_LIMIT = 58 * 1024 * 1024

TS_MIX = 256
TS_FFN = 512
FFN_BLK = 1024

P_Z = 0
P_XBC = P_Z + D
P_DT = P_XBC + SSD_XBC
P_LY = P_DT + SSD_HEADS
P_LX = P_LY + D
P_Q = P_LX + D
P_K = P_Q + RET_QK
P_V = P_K + RET_QK
P_RG = P_V + D
P_GATE = P_RG + D
P_WIDTH = P_GATE + N_BRANCHES * D


def _rmsnorm(x, g):
    return x * lax.rsqrt(jnp.mean(x * x, axis=-1, keepdims=True) + EPS) * g


def _mm(a, b):
    return jnp.dot(a.astype(BF16), b.astype(BF16), preferred_element_type=F32)


def _mm_tn(a, b):
    return lax.dot_general(a.astype(BF16), b.astype(BF16), (((0,), (0,)), ((), ())),
                           preferred_element_type=F32)


def _mm_nt(a, b):
    return lax.dot_general(a.astype(BF16), b.astype(BF16), (((1,), (1,)), ((), ())),
                           preferred_element_type=F32)


def _sigmoid(x):
    return jax.nn.sigmoid(x)


def _silu(x):
    return x * jax.nn.sigmoid(x)


def _gelu_tanh(x):
    c = math.sqrt(2.0 / math.pi)
    return 0.5 * x * (1.0 + jnp.tanh(c * (x + 0.044715 * (x * x * x))))


def _softplus(x):
    return jnp.maximum(x, 0.0) + jnp.log1p(jnp.exp(-jnp.abs(x)))


def _split3(v):
    s1 = v.astype(BF16).astype(F32)
    r1 = v - s1
    s2 = r1.astype(BF16).astype(F32)
    s3 = (r1 - s2).astype(BF16).astype(F32)
    return s1, s2, s3


def _conv_from_ext(ext_ref, w_ref, b_ref, taps, rows):
    y = ext_ref[TAIL:TAIL + rows, :] * w_ref[taps - 1:taps, :] + b_ref[...]
    for j in range(1, taps):
        y = y + ext_ref[TAIL - j:TAIL - j + rows, :] * w_ref[taps - 1 - j:taps - j, :]
    return y


def _const_spec(shape):
    nd = len(shape)
    return pl.BlockSpec(shape, lambda b, s: (0,) * nd, pipeline_mode=pl.Buffered(1))


def _tile_spec(ts, width):
    return pl.BlockSpec((None, ts, width), lambda b, s: (b, s, 0))


def _params():
    return pltpu.CompilerParams(dimension_semantics=("arbitrary", "arbitrary"),
                                vmem_limit_bytes=VMEM_LIMIT)


def _ffn_kernel(x_ref, g_ref, w_in_ref, cw_ref, cb_ref, w_out_ref, gf_ref, o_ref,
                ext_ref, tail_ref, *, final_norm):
    ts = x_ref.shape[0]
    s = pl.program_id(1)

    @pl.when(s == 0)
    def _():
        tail_ref[...] = jnp.zeros_like(tail_ref)

    x = x_ref[...]
    h = _rmsnorm(x, g_ref[...]).astype(BF16)
    acc = x
    nblk = FFN_DIM // FFN_BLK

    def in_mm(j):
        lo = j * FFN_BLK
        a = jnp.dot(h, w_in_ref[:, lo:lo + FFN_BLK], preferred_element_type=F32)
        u = jnp.dot(h, w_in_ref[:, FFN_DIM + lo:FFN_DIM + lo + FFN_BLK], preferred_element_type=F32)
        return a, u

    nxt = in_mm(0)
    for j in range(nblk):
        lo = j * FFN_BLK
        a, u = nxt
        ext_ref[0:TAIL, :] = tail_ref[:, lo:lo + FFN_BLK]
        ext_ref[TAIL:TAIL + ts, :] = a
        tail_ref[:, lo:lo + FFN_BLK] = a[ts - TAIL:ts, :]
        if j + 1 < nblk:
            nxt = in_mm(j + 1)
        ac = _conv_from_ext(ext_ref, cw_ref.at[:, lo:lo + FFN_BLK], cb_ref.at[:, lo:lo + FFN_BLK],
                            FFN_CONV, ts)
        act = (_gelu_tanh(ac) * u).astype(BF16)
        acc = acc + jnp.dot(act, w_out_ref[lo:lo + FFN_BLK, :], preferred_element_type=F32)
    if final_norm:
        acc = _rmsnorm(acc, gf_ref[...])
    o_ref[...] = acc


def _ffn_call(x, g, w_in, cw, cb, w_out, gf, final_norm):
    bsz, seq, _ = x.shape
    ts = min(TS_FFN, seq)
    return pl.pallas_call(
        functools.partial(_ffn_kernel, final_norm=final_norm),
        grid=(bsz, seq // ts),
        in_specs=[_tile_spec(ts, D), _const_spec((1, D)), _const_spec((D, 2 * FFN_DIM)),
                  _const_spec((FFN_CONV, FFN_DIM)), _const_spec((1, FFN_DIM)),
                  _const_spec((FFN_DIM, D)), _const_spec((1, D))],
        out_specs=_tile_spec(ts, D),
        out_shape=jax.ShapeDtypeStruct(x.shape, F32),
        scratch_shapes=[pltpu.VMEM((TAIL + ts, FFN_BLK), F32), pltpu.VMEM((TAIL, FFN_DIM), F32)],
        compiler_params=_params(),
        name="ffn",
    )(x, g, w_in, cw, cb, w_out, gf)


def _mixer_kernel(x_ref, g_ref, w_ref, wdt_ref,
                  scw_ref, scb_ref, dtb_ref, alog_ref, dskip_ref, nw_ref, e1_ref, e2_ref, tri_ref,
                  lcw_ref, lcb_ref, wa_ref, ba_ref, wi_ref, bi_ref, lam_ref,
                  inv_ref, wbr_ref, wo_ref, o_ref,
                  sext_ref, sstate_ref, lext_ref, carry_ref, rstate_ref):
    ts = x_ref.shape[0]
    s = pl.program_id(1)
    n_chunks = ts // CHUNK

    @pl.when(s == 0)
    def _():
        sext_ref[0:TAIL, :] = jnp.zeros((TAIL, SSD_XBC), F32)
        sstate_ref[...] = jnp.zeros_like(sstate_ref)
        lext_ref[0:TAIL, :] = jnp.zeros((TAIL, D), F32)
        carry_ref[...] = jnp.zeros_like(carry_ref)
        rstate_ref[...] = jnp.zeros_like(rstate_ref)

    x = x_ref[...]
    h = _rmsnorm(x, g_ref[...]).astype(BF16)

    def proj(lo, width):
        return lax.dot_general(h, w_ref[lo:lo + width, :], (((1,), (1,)), ((), ())),
                               preferred_element_type=F32)

    lext_ref[TAIL:TAIL + ts, :] = proj(P_LX, D)
    xb = _conv_from_ext(lext_ref, lcw_ref, lcb_ref, LRU_CONV, ts)
    lext_ref[0:TAIL, :] = lext_ref[ts:ts + TAIL, :]
    sext_ref[TAIL:TAIL + ts, :] = proj(P_XBC, SSD_XBC)

    xb16 = xb.astype(BF16)
    r_parts, i_parts = [], []
    for k in range(LRU_BLOCKS):
        blk = xb16[:, k * LRU_BLOCK:(k + 1) * LRU_BLOCK]
        r_parts.append(jnp.dot(blk, wa_ref[k], preferred_element_type=F32))
        i_parts.append(jnp.dot(blk, wi_ref[k], preferred_element_type=F32))
    z = proj(P_Z, D)
    dt_raw = lax.dot_general(h, wdt_ref[...], (((1,), (1,)), ((), ())), preferred_element_type=F32)
    dt_all = _softplus(dt_raw + dtb_ref[...])
    r = _sigmoid(jnp.concatenate(r_parts, axis=1) + ba_ref[...])
    gi = _sigmoid(jnp.concatenate(i_parts, axis=1) + bi_ref[...])
    log_a = r * ((-LRU_C) * _softplus(-lam_ref[...]))
    a = jnp.exp(log_a)
    u = jnp.sqrt(-jnp.tanh(log_a) * (1.0 + a * a)) * (gi * xb)
    q = proj(P_Q, RET_QK)
    k = proj(P_K, RET_QK)

    row = lax.broadcasted_iota(jnp.int32, (ts, D), 0)
    for d in (1, 2, 4):
        a_sh = jnp.where(row >= d, pltpu.roll(a, d, axis=0), 1.0)
        u_sh = jnp.where(row >= d, pltpu.roll(u, d, axis=0), 0.0)
        u = a * u_sh + u
        a = a * a_sh
    v = proj(P_V, D).astype(BF16)
    hk = carry_ref[...]
    hs = []
    for kb in range(ts // SUBLANES):
        hk = a[kb * SUBLANES:(kb + 1) * SUBLANES, :] * hk + u[kb * SUBLANES:(kb + 1) * SUBLANES, :]
        hs.append(hk)
    carry_ref[...] = jnp.broadcast_to(hk[SUBLANES - 1:SUBLANES, :], (SUBLANES, D))
    hseq = jnp.concatenate(hs, axis=0)

    yg = proj(P_LY, D)
    xbc = _silu(_conv_from_ext(sext_ref, scw_ref, scb_ref, SSD_CONV, ts))
    sext_ref[0:TAIL, :] = sext_ref[ts:ts + TAIL, :]
    gate_pre = [proj(P_GATE + i * D, D) for i in range(N_BRANCHES)]
    y_lru = (hseq * _gelu_tanh(yg)).astype(BF16)

    pos = (s * ts + lax.broadcasted_iota(jnp.int32, (ts, LANES), 0)).astype(F32)
    ang = pos * inv_ref[...]
    lane_t = lax.broadcasted_iota(jnp.int32, (ts, LANES), 1)
    even = (lane_t & 1) == 0
    cos_t = jnp.cos(ang)
    sin_t = jnp.where(even, -jnp.sin(ang), jnp.sin(ang))

    def rotary(t):
        swapped = jnp.where(even, pltpu.roll(t, LANES - 1, axis=1), pltpu.roll(t, 1, axis=1))
        return t * cos_t + swapped * sin_t

    qr = [rotary(q[:, hd * RET_DK:(hd + 1) * RET_DK]) for hd in range(RET_HEADS)]
    kr = [rotary(k[:, hd * RET_DK:(hd + 1) * RET_DK]) * (RET_DK ** -0.5) for hd in range(RET_HEADS)]
    br_lru = jnp.dot(y_lru, wbr_ref[1], preferred_element_type=F32)
    gr = proj(P_RG, D)

    li = lax.broadcasted_iota(jnp.int32, (CHUNK, CHUNK), 0)
    si = lax.broadcasted_iota(jnp.int32, (CHUNK, CHUNK), 1)
    causal = li >= si
    diff = (li - si).astype(F32)
    idx = li.astype(F32)
    log_gamma = [math.log(1.0 - 2.0 ** (-5.0 - hd)) for hd in range(RET_HEADS)]
    dmask = [jnp.where(diff >= 0.0, jnp.exp(jnp.maximum(diff, 0.0) * lg), 0.0) for lg in log_gamma]
    q_decay = [jnp.exp((idx + 1.0) * lg) for lg in log_gamma]
    k_decay = [jnp.exp((CHUNK - 1.0 - idx) * lg) for lg in log_gamma]
    chunk_decay = [math.exp(CHUNK * lg) for lg in log_gamma]

    a_head = -jnp.exp(alog_ref[...])
    lane = lax.broadcasted_iota(jnp.int32, (CHUNK, LANES), 1)
    lo_half = lane < SSD_HEADDIM
    gw = SSD_STATE
    hp = D // SSD_GROUPS
    heads_per_group = SSD_HEADS // SSD_GROUPS

    def pack3(t):
        s1, s2, s3 = _split3(t)
        return jnp.where(lane < SSD_HEADS, s1, jnp.where(lane < 2 * SSD_HEADS, s2, s3)).astype(BF16)

    sstates = [sstate_ref[:, g * hp:(g + 1) * hp] for g in range(SSD_GROUPS)]
    rstates = [rstate_ref[hd] for hd in range(RET_HEADS)]
    y_ssd_rows, y_ret_rows = [], []
    for c in range(n_chunks):
        rows = slice(c * CHUNK, (c + 1) * CHUNK)
        dt = dt_all[rows, :]
        da = dt * a_head
        s1, s2, s3 = _split3(da)
        tri = tri_ref[...]
        acs = (jnp.dot(tri, s1.astype(BF16), preferred_element_type=F32)
               + jnp.dot(tri, s2.astype(BF16), preferred_element_type=F32)
               + jnp.dot(tri, s3.astype(BF16), preferred_element_type=F32))
        ea = jnp.exp(acs)
        wst = jnp.exp(acs[CHUNK - 1:CHUNK, :] - acs) * dt
        exp1 = jnp.dot(jnp.concatenate([pack3(ea), pack3(wst)], axis=0), e1_ref[...],
                       preferred_element_type=F32)
        ea_x = exp1[0:CHUNK, :]
        wst_x = exp1[CHUNK:2 * CHUNK, :]
        acs_x = jnp.dot(pack3(acs), e2_ref[...], preferred_element_type=F32)
        acs_t = acs.T
        dt_t = dt.T

        xs = xbc[rows, 0:D]
        bm = xbc[rows, D:D + SSD_GROUPS * gw]
        cm = xbc[rows, D + SSD_GROUPS * gw:D + 2 * SSD_GROUPS * gw]
        xs16 = xs.astype(BF16)
        xw16 = (xs * wst_x).astype(BF16)

        y_parts = []
        for g in range(SSD_GROUPS):
            bg = bm[:, g * gw:(g + 1) * gw]
            cg = cm[:, g * gw:(g + 1) * gw]
            cb = _mm_nt(cg, bg)
            y_off = _mm(cg, sstates[g]) * ea_x[:, g * hp:(g + 1) * hp]
            dec_last = ea_x[CHUNK - 1:CHUNK, g * hp:(g + 1) * hp]
            sstates[g] = sstates[g] * dec_last + _mm_tn(bg, xw16[:, g * hp:(g + 1) * hp])
            diag_parts = []
            for pr in range(heads_per_group // 2):
                ms = []
                for hh in (2 * pr, 2 * pr + 1):
                    hd = g * heads_per_group + hh
                    seg = acs_x[:, hd * LANES:(hd + 1) * LANES] - acs_t[hd:hd + 1, :]
                    dec = jnp.exp(jnp.where(causal, seg, -jnp.inf))
                    ms.append((cb * dec * dt_t[hd:hd + 1, :]).astype(BF16))
                col = g * hp + pr * LANES
                xp = xs16[:, col:col + LANES]
                zero = jnp.zeros_like(xp)
                rhs = jnp.concatenate([jnp.where(lo_half, xp, zero), jnp.where(lo_half, zero, xp)], axis=0)
                diag_parts.append(jnp.dot(jnp.concatenate(ms, axis=1), rhs, preferred_element_type=F32))
            y_parts.append(jnp.concatenate(diag_parts, axis=1) + y_off)
        y = jnp.concatenate(y_parts, axis=1) + xs * dskip_ref[...]
        y = y * _silu(z[rows, :])
        y_ssd_rows.append(_rmsnorm(y, nw_ref[...]).astype(BF16))

        outs = []
        for hd in range(RET_HEADS):
            qh = qr[hd][rows, :]
            kh = kr[hd][rows, :]
            vh = v[rows, hd * RET_DV:(hd + 1) * RET_DV]
            scores = _mm_nt(qh, kh) * dmask[hd]
            lhs = jnp.concatenate([scores.astype(BF16), (qh * q_decay[hd]).astype(BF16)], axis=1)
            rhs = jnp.concatenate([vh, rstates[hd].astype(BF16)], axis=0)
            out = jnp.dot(lhs, rhs, preferred_element_type=F32)
            rstates[hd] = rstates[hd] * chunk_decay[hd] + _mm_tn(kh * k_decay[hd], vh)
            outs.append(out * lax.rsqrt(jnp.mean(out * out, axis=-1, keepdims=True) + EPS))
        y_ret_rows.append((jnp.concatenate(outs, axis=1) * _silu(gr[rows, :])).astype(BF16))
    for g in range(SSD_GROUPS):
        sstate_ref[:, g * hp:(g + 1) * hp] = sstates[g]
    for hd in range(RET_HEADS):
        rstate_ref[hd] = rstates[hd]

    merged = _sigmoid(gate_pre[1]) * br_lru
    br_ssd = jnp.dot(jnp.concatenate(y_ssd_rows, axis=0), wbr_ref[0], preferred_element_type=F32)
    merged = merged + _sigmoid(gate_pre[0]) * br_ssd
    br_ret = jnp.dot(jnp.concatenate(y_ret_rows, axis=0), wbr_ref[2], preferred_element_type=F32)
    merged = merged + _sigmoid(gate_pre[2]) * br_ret
    o_ref[...] = x + jnp.dot(merged.astype(BF16), wo_ref[...], preferred_element_type=F32)


def _mixer_call(x, g, w, wdt, scw, scb, dtb, alog, dskip, nw, e1, e2, tri, lcw, lcb, wa, ba, wi, bi, lam,
                inv, wbr, wo):
    bsz, seq, _ = x.shape
    ts = min(TS_MIX, seq)
    blocks = (LRU_BLOCKS, LRU_BLOCK, LRU_BLOCK)
    return pl.pallas_call(
        _mixer_kernel,
        grid=(bsz, seq // ts),
        in_specs=[_tile_spec(ts, D), _const_spec((1, D)), _const_spec((P_WIDTH, D)), _const_spec((LANES, D)),
                  _const_spec((SSD_CONV, SSD_XBC)), _const_spec((1, SSD_XBC)), _const_spec((1, LANES)),
                  _const_spec((1, LANES)), _const_spec((1, D)), _const_spec((1, D)),
                  _const_spec((LANES, D)), _const_spec((LANES, SSD_HEADS * LANES)),
                  _const_spec((CHUNK, CHUNK)),
                  _const_spec((LRU_CONV, D)), _const_spec((1, D)), _const_spec(blocks), _const_spec((1, D)),
                  _const_spec(blocks), _const_spec((1, D)), _const_spec((1, D)),
                  _const_spec((1, LANES)), _const_spec((N_BRANCHES, D, D)), _const_spec((D, D))],
        out_specs=_tile_spec(ts, D),
        out_shape=jax.ShapeDtypeStruct(x.shape, F32),
        scratch_shapes=[pltpu.VMEM((TAIL + ts, SSD_XBC), F32), pltpu.VMEM((SSD_STATE, D), F32),
                        pltpu.VMEM((TAIL + ts, D), F32), pltpu.VMEM((SUBLANES, D), F32),
                        pltpu.VMEM((RET_HEADS, RET_DK, RET_DV), F32)],
        compiler_params=_params(),
        name="mixer",
    )(x, g, w, wdt, scw, scb, dtb, alog, dskip, nw, e1, e2, tri, lcw, lcb, wa, ba, wi, bi, lam, inv, wbr, wo)


def _head_expand_matrix(width_per_head):
    r = jnp.arange(LANES)[:, None]
    c = jnp.arange(SSD_HEADS * width_per_head)[None, :]
    hit = (r < DT_SPLITS * SSD_HEADS) & ((c // width_per_head) == (r % SSD_HEADS))
    return hit.astype(BF16)


def _rep_heads(v):
    rep = jnp.tile(v.astype(F32), DT_SPLITS)
    return jnp.pad(rep, (0, LANES - rep.shape[0]))[None, :]


def _dt_rows(wt):
    rows = jnp.tile(wt[P_DT:P_DT + SSD_HEADS, :], (DT_SPLITS, 1))
    return jnp.pad(rows, ((0, LANES - DT_SPLITS * SSD_HEADS), (0, 0)))


def kernel(x, norm_mix, w_in, ssd_conv_w, ssd_conv_b, ssd_dt_bias, ssd_a_log, ssd_d, ssd_norm,
           lru_conv_w, lru_conv_b, lru_w_a, lru_b_a, lru_w_i, lru_b_i, lru_lambda,
           w_branch, w_o, norm_ffn, ffn_w_in, ffn_conv_w, ffn_conv_b, ffn_w_out, norm_final):
    depth = w_in.shape[0]
    e1 = _head_expand_matrix(SSD_HEADDIM)
    e2 = _head_expand_matrix(LANES)
    tri = (jnp.arange(CHUNK)[:, None] >= jnp.arange(CHUNK)[None, :]).astype(BF16)
    inv = 1.0 / (ROPE_BASE ** jnp.linspace(0.0, 1.0, RET_DK // 2, dtype=F32))
    inv_row = jnp.repeat(inv, 2)[None, :]

    row = lambda v: v.astype(F32)[None, :]
    w_in_t = jnp.swapaxes(w_in, 1, 2).astype(BF16)
    for l in range(depth):
        x = _mixer_call(x, row(norm_mix[l]), w_in_t[l], _dt_rows(w_in_t[l]),
                        ssd_conv_w[l], row(ssd_conv_b[l]), _rep_heads(ssd_dt_bias[l]),
                        _rep_heads(ssd_a_log[l]), row(jnp.repeat(ssd_d[l], SSD_HEADDIM)), row(ssd_norm[l]),
                        e1, e2, tri,
                        lru_conv_w[l], row(lru_conv_b[l]), lru_w_a[l].astype(BF16), row(lru_b_a[l]),
                        lru_w_i[l].astype(BF16), row(lru_b_i[l]), row(lru_lambda[l]),
                        inv_row, w_branch[l].astype(BF16), w_o[l].astype(BF16))
        x = _ffn_call(x, row(norm_ffn[l]), ffn_w_in[l].astype(BF16), ffn_conv_w[l], row(ffn_conv_b[l]),
                      ffn_w_out[l].astype(BF16), row(norm_final), final_norm=(l == depth - 1))
    return x
```

```python
import functools
import math

import jax
import jax.numpy as jnp
from jax import lax
from jax.experimental import pallas as pl
from jax.experimental.pallas import tpu as pltpu

F32 = jnp.float32
BF16 = jnp.bfloat16

EPS = 1e-6
CHUNK = 128
D = 1024
SSD_HEADS = 16
SSD_HEADDIM = 64
SSD_GROUPS = 2
SSD_STATE = 128
SSD_CONV = 4
SSD_BC = 2 * SSD_GROUPS * SSD_STATE
SSD_XBC = D + SSD_BC
LRU_BLOCKS = 4
LRU_BLOCK = D // LRU_BLOCKS
LRU_CONV = 4
LRU_C = 8.0
RET_HEADS = 4
RET_DK = 128
RET_DV = 256
RET_QK = RET_HEADS * RET_DK
ROPE_BASE = 10000.0
FFN_DIM = 3 * D
FFN_CONV = 3
N_BRANCHES = 3
LANES = 128
SUBLANES = 8
TAIL = SUBLANES
DT_SPLITS = 3
VMEM_LIMIT = 58 * 1024 * 1024

TS_MIX = 256
TS_FFN = 512
FFN_BLK = 1536

P_Z = 0
P_XBC = P_Z + D
P_DT = P_XBC + SSD_XBC
P_LY = P_DT + SSD_HEADS
P_LX = P_LY + D
P_Q = P_LX + D
P_K = P_Q + RET_QK
P_V = P_K + RET_QK
P_RG = P_V + D
P_GATE = P_RG + D
P_WIDTH = P_GATE + N_BRANCHES * D


def _rmsnorm(x, g):
    return x * lax.rsqrt(jnp.mean(x * x, axis=-1, keepdims=True) + EPS) * g


def _mm(a, b):
    return jnp.dot(a.astype(BF16), b.astype(BF16), preferred_element_type=F32)


def _mm_tn(a, b):
    return lax.dot_general(a.astype(BF16), b.astype(BF16), (((0,), (0,)), ((), ())),
                           preferred_element_type=F32)


def _mm_nt(a, b):
    return lax.dot_general(a.astype(BF16), b.astype(BF16), (((1,), (1,)), ((), ())),
                           preferred_element_type=F32)


def _sigmoid(x):
    return jax.nn.sigmoid(x)


def _silu(x):
    return x * jax.nn.sigmoid(x)


def _gelu_tanh(x):
    c = math.sqrt(2.0 / math.pi)
    return 0.5 * x * (1.0 + jnp.tanh(c * (x + 0.044715 * (x * x * x))))


def _softplus(x):
    return jnp.maximum(x, 0.0) + jnp.log1p(jnp.exp(-jnp.abs(x)))


def _sqrt_nonneg(v):
    return jnp.where(v > 0.0, v * lax.rsqrt(v), 0.0)


def _split3(v):
    s1 = v.astype(BF16).astype(F32)
    r1 = v - s1
    s2 = r1.astype(BF16).astype(F32)
    s3 = (r1 - s2).astype(BF16).astype(F32)
    return s1, s2, s3


def _conv_from_ext(ext_ref, w_ref, b_ref, taps, rows):
    y = ext_ref[TAIL:TAIL + rows, :] * w_ref[taps - 1:taps, :] + b_ref[...]
    for j in range(1, taps):
        y = y + ext_ref[TAIL - j:TAIL - j + rows, :] * w_ref[taps - 1 - j:taps - j, :]
    return y


def _const_spec(shape):
    nd = len(shape)
    return pl.BlockSpec(shape, lambda b, s: (0,) * nd, pipeline_mode=pl.Buffered(1))


def _layer_spec(l, shape):
    nd = len(shape)
    return pl.BlockSpec((None,) + tuple(shape), lambda b, s: (l,) + (0,) * nd,
                        pipeline_mode=pl.Buffered(1))


def _tile_spec(ts, width):
    return pl.BlockSpec((None, ts, width), lambda b, s: (b, s, 0))


def _params():
    return pltpu.CompilerParams(dimension_semantics=("arbitrary", "arbitrary"),
                                vmem_limit_bytes=VMEM_LIMIT)


def _ffn_kernel(x_ref, g_ref, w_in_ref, cw_ref, cb_ref, w_out_ref, gf_ref, o_ref,
                ext_ref, tail_ref, *, final_norm):
    ts = x_ref.shape[0]
    s = pl.program_id(1)

    @pl.when(s == 0)
    def _():
        tail_ref[...] = jnp.zeros_like(tail_ref)

    x = x_ref[...]
    h = _rmsnorm(x, g_ref[...]).astype(BF16)
    acc = x
    nblk = FFN_DIM // FFN_BLK

    def in_mm(j):
        lo = j * FFN_BLK
        a = jnp.dot(h, w_in_ref[:, lo:lo + FFN_BLK], preferred_element_type=F32)
        u = jnp.dot(h, w_in_ref[:, FFN_DIM + lo:FFN_DIM + lo + FFN_BLK], preferred_element_type=F32)
        return a, u

    nxt = in_mm(0)
    for j in range(nblk):
        lo = j * FFN_BLK
        a, u = nxt
        ext_ref[0:TAIL, :] = tail_ref[:, lo:lo + FFN_BLK]
        ext_ref[TAIL:TAIL + ts, :] = a
        tail_ref[:, lo:lo + FFN_BLK] = a[ts - TAIL:ts, :]
        if j + 1 < nblk:
            nxt = in_mm(j + 1)
        ac = _conv_from_ext(ext_ref, cw_ref.at[:, lo:lo + FFN_BLK], cb_ref.at[:, lo:lo + FFN_BLK],
                            FFN_CONV, ts)
        act = (_gelu_tanh(ac) * u).astype(BF16)
        acc = acc + jnp.dot(act, w_out_ref[lo:lo + FFN_BLK, :], preferred_element_type=F32)
    if final_norm:
        acc = _rmsnorm(acc, gf_ref[...])
    o_ref[...] = acc


def _ffn_call(l, x, g, w_in, cw, cb, w_out, gf, final_norm):
    bsz, seq, _ = x.shape
    ts = min(TS_FFN, seq)
    ls = functools.partial(_layer_spec, l)
    return pl.pallas_call(
        functools.partial(_ffn_kernel, final_norm=final_norm),
        grid=(bsz, seq // ts),
        in_specs=[_tile_spec(ts, D), ls((1, D)), ls((D, 2 * FFN_DIM)),
                  ls((FFN_CONV, FFN_DIM)), ls((1, FFN_DIM)),
                  ls((FFN_DIM, D)), _const_spec((1, D))],
        out_specs=_tile_spec(ts, D),
        out_shape=jax.ShapeDtypeStruct(x.shape, F32),
        scratch_shapes=[pltpu.VMEM((TAIL + ts, FFN_BLK), F32), pltpu.VMEM((TAIL, FFN_DIM), F32)],
        compiler_params=_params(),
        name="ffn",
    )(x, g, w_in, cw, cb, w_out, gf)


def _mixer_kernel(x_ref, g_ref, w_ref, wdt_ref,
                  scw_ref, scb_ref, dtb_ref, alog_ref, dskip_ref, nw_ref, e1_ref, e2_ref, tri_ref,
                  lcw_ref, lcb_ref, wa_ref, ba_ref, wi_ref, bi_ref, lam_ref,
                  inv_ref, wbr_ref, wo_ref, o_ref,
                  sext_ref, sstate_ref, lext_ref, carry_ref, rstate_ref):
    ts = x_ref.shape[0]
    s = pl.program_id(1)
    n_chunks = ts // CHUNK

    @pl.when(s == 0)
    def _():
        sext_ref[0:TAIL, :] = jnp.zeros((TAIL, SSD_XBC), F32)
        sstate_ref[...] = jnp.zeros_like(sstate_ref)
        lext_ref[0:TAIL, :] = jnp.zeros((TAIL, D), F32)
        carry_ref[...] = jnp.zeros_like(carry_ref)
        rstate_ref[...] = jnp.zeros_like(rstate_ref)

    x = x_ref[...]
    h = _rmsnorm(x, g_ref[...]).astype(BF16)

    def proj(lo, width):
        return lax.dot_general(h, w_ref[lo:lo + width, :], (((1,), (1,)), ((), ())),
                               preferred_element_type=F32)

    lext_ref[TAIL:TAIL + ts, :] = proj(P_LX, D)
    xb = _conv_from_ext(lext_ref, lcw_ref, lcb_ref, LRU_CONV, ts)
    lext_ref[0:TAIL, :] = lext_ref[ts:ts + TAIL, :]
    sext_ref[TAIL:TAIL + ts, :] = proj(P_XBC, SSD_XBC)

    xb16 = xb.astype(BF16)
    r_parts, i_parts = [], []
    for k in range(LRU_BLOCKS):
        blk = xb16[:, k * LRU_BLOCK:(k + 1) * LRU_BLOCK]
        r_parts.append(jnp.dot(blk, wa_ref[k], preferred_element_type=F32))
        i_parts.append(jnp.dot(blk, wi_ref[k], preferred_element_type=F32))
    z = proj(P_Z, D)
    dt_raw = lax.dot_general(h, wdt_ref[...], (((1,), (1,)), ((), ())), preferred_element_type=F32)
    dt_all = _softplus(dt_raw + dtb_ref[...])
    r = _sigmoid(jnp.concatenate(r_parts, axis=1) + ba_ref[...])
    gi = _sigmoid(jnp.concatenate(i_parts, axis=1) + bi_ref[...])
    log_a = r * ((-LRU_C) * _softplus(-lam_ref[...]))
    a = jnp.exp(log_a)
    u = _sqrt_nonneg(-jnp.tanh(log_a) * (1.0 + a * a)) * (gi * xb)
    q = proj(P_Q, RET_QK)
    k = proj(P_K, RET_QK)

    row = lax.broadcasted_iota(jnp.int32, (ts, D), 0)
    for d in (1, 2, 4):
        a_sh = jnp.where(row >= d, pltpu.roll(a, d, axis=0), 1.0)
        u_sh = jnp.where(row >= d, pltpu.roll(u, d, axis=0), 0.0)
        u = a * u_sh + u
        a = a * a_sh
    v = proj(P_V, D).astype(BF16)
    hk = carry_ref[...]
    hs = []
    for kb in range(ts // SUBLANES):
        hk = a[kb * SUBLANES:(kb + 1) * SUBLANES, :] * hk + u[kb * SUBLANES:(kb + 1) * SUBLANES, :]
        hs.append(hk)
    carry_ref[...] = jnp.broadcast_to(hk[SUBLANES - 1:SUBLANES, :], (SUBLANES, D))
    hseq = jnp.concatenate(hs, axis=0)

    yg = proj(P_LY, D)
    xbc = _silu(_conv_from_ext(sext_ref, scw_ref, scb_ref, SSD_CONV, ts))
    sext_ref[0:TAIL, :] = sext_ref[ts:ts + TAIL, :]
    gate_pre = [proj(P_GATE + i * D, D) for i in range(N_BRANCHES)]
    y_lru = (hseq * _gelu_tanh(yg)).astype(BF16)

    pos = (s * ts + lax.broadcasted_iota(jnp.int32, (ts, LANES), 0)).astype(F32)
    ang = pos * inv_ref[...]
    lane_t = lax.broadcasted_iota(jnp.int32, (ts, LANES), 1)
    even = (lane_t & 1) == 0
    cos_t = jnp.cos(ang)
    sin_t = jnp.where(even, -jnp.sin(ang), jnp.sin(ang))

    def rotary(t):
        swapped = jnp.where(even, pltpu.roll(t, LANES - 1, axis=1), pltpu.roll(t, 1, axis=1))
        return t * cos_t + swapped * sin_t

    qr = [rotary(q[:, hd * RET_DK:(hd + 1) * RET_DK]) for hd in range(RET_HEADS)]
    kr = [rotary(k[:, hd * RET_DK:(hd + 1) * RET_DK]) * (RET_DK ** -0.5) for hd in range(RET_HEADS)]
    br_lru = jnp.dot(y_lru, wbr_ref[1], preferred_element_type=F32)
    gr = proj(P_RG, D)

    li = lax.broadcasted_iota(jnp.int32, (CHUNK, CHUNK), 0)
    si = lax.broadcasted_iota(jnp.int32, (CHUNK, CHUNK), 1)
    causal = li >= si
    diff = (li - si).astype(F32)
    idx = li.astype(F32)
    log_gamma = [math.log(1.0 - 2.0 ** (-5.0 - hd)) for hd in range(RET_HEADS)]
    dmask = [jnp.where(diff >= 0.0, jnp.exp(jnp.maximum(diff, 0.0) * lg), 0.0) for lg in log_gamma]
    q_decay = [jnp.exp((idx + 1.0) * lg) for lg in log_gamma]
    k_decay = [jnp.exp((CHUNK - 1.0 - idx) * lg) for lg in log_gamma]
    chunk_decay = [math.exp(CHUNK * lg) for lg in log_gamma]

    a_head = -jnp.exp(alog_ref[...])
    lane = lax.broadcasted_iota(jnp.int32, (CHUNK, LANES), 1)
    lo_half = lane < SSD_HEADDIM
    gw = SSD_STATE
    hp = D // SSD_GROUPS
    heads_per_group = SSD_HEADS // SSD_GROUPS

    def pack3(t):
        s1, s2, s3 = _split3(t)
        return jnp.where(lane < SSD_HEADS, s1, jnp.where(lane < 2 * SSD_HEADS, s2, s3)).astype(BF16)

    sstates = [sstate_ref[:, g * hp:(g + 1) * hp] for g in range(SSD_GROUPS)]
    rstates = [rstate_ref[hd] for hd in range(RET_HEADS)]
    y_ssd_rows, y_ret_rows = [], []
    for c in range(n_chunks):
        rows = slice(c * CHUNK, (c + 1) * CHUNK)
        dt = dt_all[rows, :]
        da = dt * a_head
        s1, s2, s3 = _split3(da)
        tri = tri_ref[...]
        acs = (jnp.dot(tri, s1.astype(BF16), preferred_element_type=F32)
               + jnp.dot(tri, s2.astype(BF16), preferred_element_type=F32)
               + jnp.dot(tri, s3.astype(BF16), preferred_element_type=F32))
        ea = jnp.exp(acs)
        wst = jnp.exp(acs[CHUNK - 1:CHUNK, :] - acs) * dt
        exp1 = jnp.dot(jnp.concatenate([pack3(ea), pack3(wst)], axis=0), e1_ref[...],
                       preferred_element_type=F32)
        ea_x = exp1[0:CHUNK, :]
        wst_x = exp1[CHUNK:2 * CHUNK, :]
        acs_x = jnp.dot(pack3(acs), e2_ref[...], preferred_element_type=F32)
        acs_t = acs.T
        dt_t = dt.T

        xs = xbc[rows, 0:D]
        bm = xbc[rows, D:D + SSD_GROUPS * gw]
        cm = xbc[rows, D + SSD_GROUPS * gw:D + 2 * SSD_GROUPS * gw]
        xs16 = xs.astype(BF16)
        xw16 = (xs * wst_x).astype(BF16)

        y_parts = []
        for g in range(SSD_GROUPS):
            bg = bm[:, g * gw:(g + 1) * gw]
            cg = cm[:, g * gw:(g + 1) * gw]
            cb = _mm_nt(cg, bg)
            y_off = _mm(cg, sstates[g]) * ea_x[:, g * hp:(g + 1) * hp]
            dec_last = ea_x[CHUNK - 1:CHUNK, g * hp:(g + 1) * hp]
            sstates[g] = sstates[g] * dec_last + _mm_tn(bg, xw16[:, g * hp:(g + 1) * hp])
            diag_parts = []
            for pr in range(heads_per_group // 2):
                ms = []
                for hh in (2 * pr, 2 * pr + 1):
                    hd = g * heads_per_group + hh
                    seg = acs_x[:, hd * LANES:(hd + 1) * LANES] - acs_t[hd:hd + 1, :]
                    dec = jnp.exp(jnp.where(causal, seg, -jnp.inf))
                    ms.append((cb * dec * dt_t[hd:hd + 1, :]).astype(BF16))
                col = g * hp + pr * LANES
                xp = xs16[:, col:col + LANES]
                zero = jnp.zeros_like(xp)
                rhs = jnp.concatenate([jnp.where(lo_half, xp, zero), jnp.where(lo_half, zero, xp)], axis=0)
                diag_parts.append(jnp.dot(jnp.concatenate(ms, axis=1), rhs, preferred_element_type=F32))
            y_parts.append(jnp.concatenate(diag_parts, axis=1) + y_off)
        y = jnp.concatenate(y_parts, axis=1) + xs * dskip_ref[...]
        y = y * _silu(z[rows, :])
        y_ssd_rows.append(_rmsnorm(y, nw_ref[...]).astype(BF16))

        outs = []
        for hd in range(RET_HEADS):
            qh = qr[hd][rows, :]
            kh = kr[hd][rows, :]
            vh = v[rows, hd * RET_DV:(hd + 1) * RET_DV]
            scores = _mm_nt(qh, kh) * dmask[hd]
            lhs = jnp.concatenate([scores.astype(BF16), (qh * q_decay[hd]).astype(BF16)], axis=1)
            rhs = jnp.concatenate([vh, rstates[hd].astype(BF16)], axis=0)
            out = jnp.dot(lhs, rhs, preferred_element_type=F32)
            rstates[hd] = rstates[hd] * chunk_decay[hd] + _mm_tn(kh * k_decay[hd], vh)
            outs.append(out * lax.rsqrt(jnp.mean(out * out, axis=-1, keepdims=True) + EPS))
        y_ret_rows.append((jnp.concatenate(outs, axis=1) * _silu(gr[rows, :])).astype(BF16))
    for g in range(SSD_GROUPS):
        sstate_ref[:, g * hp:(g + 1) * hp] = sstates[g]
    for hd in range(RET_HEADS):
        rstate_ref[hd] = rstates[hd]

    merged = _sigmoid(gate_pre[1]) * br_lru
    br_ssd = jnp.dot(jnp.concatenate(y_ssd_rows, axis=0), wbr_ref[0], preferred_element_type=F32)
    merged = merged + _sigmoid(gate_pre[0]) * br_ssd
    br_ret = jnp.dot(jnp.concatenate(y_ret_rows, axis=0), wbr_ref[2], preferred_element_type=F32)
    merged = merged + _sigmoid(gate_pre[2]) * br_ret
    o_ref[...] = x + jnp.dot(merged.astype(BF16), wo_ref[...], preferred_element_type=F32)


def _mixer_call(l, x, g, w, wdt, scw, scb, dtb, alog, dskip, nw, e1, e2, tri, lcw, lcb, wa, ba, wi, bi,
                lam, inv, wbr, wo):
    bsz, seq, _ = x.shape
    ts = min(TS_MIX, seq)
    blocks = (LRU_BLOCKS, LRU_BLOCK, LRU_BLOCK)
    ls = functools.partial(_layer_spec, l)
    return pl.pallas_call(
        _mixer_kernel,
        grid=(bsz, seq // ts),
        in_specs=[_tile_spec(ts, D), ls((1, D)), ls((P_WIDTH, D)), ls((LANES, D)),
                  ls((SSD_CONV, SSD_XBC)), ls((1, SSD_XBC)), ls((1, LANES)),
                  ls((1, LANES)), ls((1, D)), ls((1, D)),
                  _const_spec((LANES, D)), _const_spec((LANES, SSD_HEADS * LANES)),
                  _const_spec((CHUNK, CHUNK)),
                  ls((LRU_CONV, D)), ls((1, D)), ls(blocks), ls((1, D)),
                  ls(blocks), ls((1, D)), ls((1, D)),
                  _const_spec((1, LANES)), ls((N_BRANCHES, D, D)), ls((D, D))],
        out_specs=_tile_spec(ts, D),
        out_shape=jax.ShapeDtypeStruct(x.shape, F32),
        scratch_shapes=[pltpu.VMEM((TAIL + ts, SSD_XBC), F32), pltpu.VMEM((SSD_STATE, D), F32),
                        pltpu.VMEM((TAIL + ts, D), F32), pltpu.VMEM((SUBLANES, D), F32),
                        pltpu.VMEM((RET_HEADS, RET_DK, RET_DV), F32)],
        compiler_params=_params(),
        name="mixer",
    )(x, g, w, wdt, scw, scb, dtb, alog, dskip, nw, e1, e2, tri, lcw, lcb, wa, ba, wi, bi, lam, inv, wbr, wo)


def _head_expand_matrix(width_per_head):
    r = jnp.arange(LANES)[:, None]
    c = jnp.arange(SSD_HEADS * width_per_head)[None, :]
    hit = (r < DT_SPLITS * SSD_HEADS) & ((c // width_per_head) == (r % SSD_HEADS))
    return hit.astype(BF16)


def _rep_heads(v):
    rep = jnp.tile(v.astype(F32), (1, DT_SPLITS))
    return jnp.pad(rep, ((0, 0), (0, LANES - rep.shape[1])))[:, None, :]


def _dt_rows(wt):
    rows = jnp.tile(wt[:, P_DT:P_DT + SSD_HEADS, :], (1, DT_SPLITS, 1))
    return jnp.pad(rows, ((0, 0), (0, LANES - DT_SPLITS * SSD_HEADS), (0, 0)))


def kernel(x, norm_mix, w_in, ssd_conv_w, ssd_conv_b, ssd_dt_bias, ssd_a_log, ssd_d, ssd_norm,
           lru_conv_w, lru_conv_b, lru_w_a, lru_b_a, lru_w_i, lru_b_i, lru_lambda,
           w_branch, w_o, norm_ffn, ffn_w_in, ffn_conv_w, ffn_conv_b, ffn_w_out, norm_final):
    depth = w_in.shape[0]
    e1 = _head_expand_matrix(SSD_HEADDIM)
    e2 = _head_expand_matrix(LANES)
    tri = (jnp.arange(CHUNK)[:, None] >= jnp.arange(CHUNK)[None, :]).astype(BF16)
    inv = 1.0 / (ROPE_BASE ** jnp.linspace(0.0, 1.0, RET_DK // 2, dtype=F32))
    inv_row = jnp.repeat(inv, 2)[None, :]

    rows = lambda v: v.astype(F32)[:, None, :]
    w_in_t = jnp.swapaxes(w_in, 1, 2).astype(BF16)
    mixer_args = (rows(norm_mix), w_in_t, _dt_rows(w_in_t),
                  ssd_conv_w, rows(ssd_conv_b), _rep_heads(ssd_dt_bias), _rep_heads(ssd_a_log),
                  rows(jnp.repeat(ssd_d, SSD_HEADDIM, axis=1)), rows(ssd_norm), e1, e2, tri,
                  lru_conv_w, rows(lru_conv_b), lru_w_a.astype(BF16), rows(lru_b_a),
                  lru_w_i.astype(BF16), rows(lru_b_i), rows(lru_lambda),
                  inv_row, w_branch.astype(BF16), w_o.astype(BF16))
    ffn_args = (rows(norm_ffn), ffn_w_in.astype(BF16), ffn_conv_w, rows(ffn_conv_b),
                ffn_w_out.astype(BF16), norm_final.astype(F32)[None, :])
    for l in range(depth):
        x = _mixer_call(l, x, *mixer_args)
        x = _ffn_call(l, x, *ffn_args, final_norm=(l == depth - 1))
    return x
```

```python
import functools
import math

import jax
import jax.numpy as jnp
from jax import lax
from jax.experimental import pallas as pl
from jax.experimental.pallas import tpu as pltpu

F32 = jnp.float32
BF16 = jnp.bfloat16

EPS = 1e-6
CHUNK = 128
D = 1024
SSD_HEADS = 16
SSD_HEADDIM = 64
SSD_GROUPS = 2
SSD_STATE = 128
SSD_CONV = 4
SSD_BC = 2 * SSD_GROUPS * SSD_STATE
SSD_XBC = D + SSD_BC
LRU_BLOCKS = 4
LRU_BLOCK = D // LRU_BLOCKS
LRU_CONV = 4
LRU_C = 8.0
RET_HEADS = 4
RET_DK = 128
RET_DV = 256
RET_QK = RET_HEADS * RET_DK
ROPE_BASE = 10000.0
FFN_DIM = 3 * D
FFN_CONV = 3
N_BRANCHES = 3
LANES = 128
SUBLANES = 8
TAIL = SUBLANES
DT_SPLITS = 3
VMEM_LIMIT = 58 * 1024 * 1024

TS_MIX = 256
TS_FFN = 512
FFN_BLK = 3072

P_Z = 0
P_XBC = P_Z + D
P_DT = P_XBC + SSD_XBC
P_LY = P_DT + SSD_HEADS
P_LX = P_LY + D
P_Q = P_LX + D
P_K = P_Q + RET_QK
P_V = P_K + RET_QK
P_RG = P_V + D
P_GATE = P_RG + D
P_WIDTH = P_GATE + N_BRANCHES * D


def _rmsnorm(x, g):
    return x * lax.rsqrt(jnp.mean(x * x, axis=-1, keepdims=True) + EPS) * g


def _mm(a, b):
    return jnp.dot(a.astype(BF16), b.astype(BF16), preferred_element_type=F32)


def _mm_tn(a, b):
    return lax.dot_general(a.astype(BF16), b.astype(BF16), (((0,), (0,)), ((), ())),
                           preferred_element_type=F32)


def _mm_nt(a, b):
    return lax.dot_general(a.astype(BF16), b.astype(BF16), (((1,), (1,)), ((), ())),
                           preferred_element_type=F32)


def _sigmoid(x):
    return jax.nn.sigmoid(x)


def _silu(x):
    return x * jax.nn.sigmoid(x)


def _gelu_tanh(x):
    c = math.sqrt(2.0 / math.pi)
    return 0.5 * x * (1.0 + jnp.tanh(c * (x + 0.044715 * (x * x * x))))


def _softplus(x):
    return jnp.maximum(x, 0.0) + jnp.log1p(jnp.exp(-jnp.abs(x)))


def _sqrt_nonneg(v):
    return jnp.where(v > 0.0, v * lax.rsqrt(v), 0.0)


def _split3(v):
    s1 = v.astype(BF16).astype(F32)
    r1 = v - s1
    s2 = r1.astype(BF16).astype(F32)
    s3 = (r1 - s2).astype(BF16).astype(F32)
    return s1, s2, s3


def _conv_from_ext(ext_ref, w_ref, b_ref, taps, rows):
    y = ext_ref[TAIL:TAIL + rows, :] * w_ref[taps - 1:taps, :] + b_ref[...]
    for j in range(1, taps):
        y = y + ext_ref[TAIL - j:TAIL - j + rows, :] * w_ref[taps - 1 - j:taps - j, :]
    return y


def _const_spec(shape):
    nd = len(shape)
    return pl.BlockSpec(shape, lambda b, s: (0,) * nd, pipeline_mode=pl.Buffered(1))


def _layer_spec(l, shape):
    nd = len(shape)
    return pl.BlockSpec((None,) + tuple(shape), lambda b, s: (l,) + (0,) * nd,
                        pipeline_mode=pl.Buffered(1))


def _tile_spec(ts, width):
    return pl.BlockSpec((None, ts, width), lambda b, s: (b, s, 0))


def _params():
    return pltpu.CompilerParams(dimension_semantics=("arbitrary", "arbitrary"),
                                vmem_limit_bytes=VMEM_LIMIT)


def _ffn_kernel(x_ref, g_ref, w_in_ref, cw_ref, cb_ref, w_out_ref, gf_ref, o_ref,
                ext_ref, tail_ref, *, final_norm):
    ts = x_ref.shape[0]
    s = pl.program_id(1)

    @pl.when(s == 0)
    def _():
        tail_ref[...] = jnp.zeros_like(tail_ref)

    x = x_ref[...]
    h = _rmsnorm(x, g_ref[...]).astype(BF16)
    acc = x
    nblk = FFN_DIM // FFN_BLK

    def in_mm(j):
        lo = j * FFN_BLK
        a = jnp.dot(h, w_in_ref[:, lo:lo + FFN_BLK], preferred_element_type=F32)
        u = jnp.dot(h, w_in_ref[:, FFN_DIM + lo:FFN_DIM + lo + FFN_BLK], preferred_element_type=F32)
        return a, u

    nxt = in_mm(0)
    for j in range(nblk):
        lo = j * FFN_BLK
        a, u = nxt
        ext_ref[0:TAIL, :] = tail_ref[:, lo:lo + FFN_BLK]
        ext_ref[TAIL:TAIL + ts, :] = a
        tail_ref[:, lo:lo + FFN_BLK] = a[ts - TAIL:ts, :]
        if j + 1 < nblk:
            nxt = in_mm(j + 1)
        ac = _conv_from_ext(ext_ref, cw_ref.at[:, lo:lo + FFN_BLK], cb_ref.at[:, lo:lo + FFN_BLK],
                            FFN_CONV, ts)
        act = (_gelu_tanh(ac) * u).astype(BF16)
        acc = acc + jnp.dot(act, w_out_ref[lo:lo + FFN_BLK, :], preferred_element_type=F32)
    if final_norm:
        acc = _rmsnorm(acc, gf_ref[...])
    o_ref[...] = acc


def _ffn_call(l, x, g, w_in, cw, cb, w_out, gf, final_norm):
    bsz, seq, _ = x.shape
    ts = min(TS_FFN, seq)
    ls = functools.partial(_layer_spec, l)
    return pl.pallas_call(
        functools.partial(_ffn_kernel, final_norm=final_norm),
        grid=(bsz, seq // ts),
        in_specs=[_tile_spec(ts, D), ls((1, D)), ls((D, 2 * FFN_DIM)),
                  ls((FFN_CONV, FFN_DIM)), ls((1, FFN_DIM)),
                  ls((FFN_DIM, D)), _const_spec((1, D))],
        out_specs=_tile_spec(ts, D),
        out_shape=jax.ShapeDtypeStruct(x.shape, F32),
        scratch_shapes=[pltpu.VMEM((TAIL + ts, FFN_BLK), F32), pltpu.VMEM((TAIL, FFN_DIM), F32)],
        compiler_params=_params(),
        name="ffn",
    )(x, g, w_in, cw, cb, w_out, gf)


def _mixer_kernel(x_ref, g_ref, w_ref, wdt_ref,
                  scw_ref, scb_ref, dtb_ref, alog_ref, dskip_ref, nw_ref, e1_ref, tri_ref,
                  lcw_ref, lcb_ref, wa_ref, ba_ref, wi_ref, bi_ref, lam_ref,
                  inv_ref, wbr_ref, wo_ref, o_ref,
                  sext_ref, sstate_ref, lext_ref, carry_ref, rstate_ref):
    ts = x_ref.shape[0]
    s = pl.program_id(1)
    n_chunks = ts // CHUNK

    @pl.when(s == 0)
    def _():
        sext_ref[0:TAIL, :] = jnp.zeros((TAIL, SSD_XBC), F32)
        sstate_ref[...] = jnp.zeros_like(sstate_ref)
        lext_ref[0:TAIL, :] = jnp.zeros((TAIL, D), F32)
        carry_ref[...] = jnp.zeros_like(carry_ref)
        rstate_ref[...] = jnp.zeros_like(rstate_ref)

    x = x_ref[...]
    h = _rmsnorm(x, g_ref[...]).astype(BF16)

    def proj(lo, width):
        return lax.dot_general(h, w_ref[lo:lo + width, :], (((1,), (1,)), ((), ())),
                               preferred_element_type=F32)

    lext_ref[TAIL:TAIL + ts, :] = proj(P_LX, D)
    xb = _conv_from_ext(lext_ref, lcw_ref, lcb_ref, LRU_CONV, ts)
    lext_ref[0:TAIL, :] = lext_ref[ts:ts + TAIL, :]
    sext_ref[TAIL:TAIL + ts, :] = proj(P_XBC, SSD_XBC)

    xb16 = xb.astype(BF16)
    r_parts, i_parts = [], []
    for k in range(LRU_BLOCKS):
        blk = xb16[:, k * LRU_BLOCK:(k + 1) * LRU_BLOCK]
        r_parts.append(jnp.dot(blk, wa_ref[k], preferred_element_type=F32))
        i_parts.append(jnp.dot(blk, wi_ref[k], preferred_element_type=F32))
    z = proj(P_Z, D)
    dt_raw = lax.dot_general(h, wdt_ref[...], (((1,), (1,)), ((), ())), preferred_element_type=F32)
    dt_all = _softplus(dt_raw + dtb_ref[...])
    r = _sigmoid(jnp.concatenate(r_parts, axis=1) + ba_ref[...])
    gi = _sigmoid(jnp.concatenate(i_parts, axis=1) + bi_ref[...])
    log_a = r * ((-LRU_C) * _softplus(-lam_ref[...]))
    a = jnp.exp(log_a)
    u = _sqrt_nonneg(-jnp.tanh(log_a) * (1.0 + a * a)) * (gi * xb)
    q = proj(P_Q, RET_QK)
    k = proj(P_K, RET_QK)

    row = lax.broadcasted_iota(jnp.int32, (ts, D), 0)
    for d in (1, 2, 4):
        a_sh = jnp.where(row >= d, pltpu.roll(a, d, axis=0), 1.0)
        u_sh = jnp.where(row >= d, pltpu.roll(u, d, axis=0), 0.0)
        u = a * u_sh + u
        a = a * a_sh
    v = proj(P_V, D).astype(BF16)
    hk = carry_ref[...]
    hs = []
    for kb in range(ts // SUBLANES):
        hk = a[kb * SUBLANES:(kb + 1) * SUBLANES, :] * hk + u[kb * SUBLANES:(kb + 1) * SUBLANES, :]
        hs.append(hk)
    carry_ref[...] = jnp.broadcast_to(hk[SUBLANES - 1:SUBLANES, :], (SUBLANES, D))
    hseq = jnp.concatenate(hs, axis=0)

    yg = proj(P_LY, D)
    xbc = _silu(_conv_from_ext(sext_ref, scw_ref, scb_ref, SSD_CONV, ts))
    sext_ref[0:TAIL, :] = sext_ref[ts:ts + TAIL, :]
    gate_pre = [proj(P_GATE + i * D, D) for i in range(N_BRANCHES)]
    y_lru = (hseq * _gelu_tanh(yg)).astype(BF16)

    pos = (s * ts + lax.broadcasted_iota(jnp.int32, (ts, LANES), 0)).astype(F32)
    ang = pos * inv_ref[...]
    lane_t = lax.broadcasted_iota(jnp.int32, (ts, LANES), 1)
    even = (lane_t & 1) == 0
    cos_t = jnp.cos(ang)
    sin_t = jnp.where(even, -jnp.sin(ang), jnp.sin(ang))

    def rotary(t):
        swapped = jnp.where(even, pltpu.roll(t, LANES - 1, axis=1), pltpu.roll(t, 1, axis=1))
        return t * cos_t + swapped * sin_t

    qr = [rotary(q[:, hd * RET_DK:(hd + 1) * RET_DK]) for hd in range(RET_HEADS)]
    kr = [rotary(k[:, hd * RET_DK:(hd + 1) * RET_DK]) * (RET_DK ** -0.5) for hd in range(RET_HEADS)]
    br_lru = jnp.dot(y_lru, wbr_ref[1], preferred_element_type=F32)
    gr = proj(P_RG, D)

    li = lax.broadcasted_iota(jnp.int32, (CHUNK, CHUNK), 0)
    si = lax.broadcasted_iota(jnp.int32, (CHUNK, CHUNK), 1)
    causal = li >= si
    diff = (li - si).astype(F32)
    idx = li.astype(F32)
    log_gamma = [math.log(1.0 - 2.0 ** (-5.0 - hd)) for hd in range(RET_HEADS)]
    dmask = [jnp.where(diff >= 0.0, jnp.exp(jnp.maximum(diff, 0.0) * lg), 0.0) for lg in log_gamma]
    q_decay = [jnp.exp((idx + 1.0) * lg) for lg in log_gamma]
    k_decay = [jnp.exp((CHUNK - 1.0 - idx) * lg) for lg in log_gamma]
    chunk_decay = [math.exp(CHUNK * lg) for lg in log_gamma]

    a_head = -jnp.exp(alog_ref[...])
    lane = lax.broadcasted_iota(jnp.int32, (CHUNK, LANES), 1)
    lo_half = lane < SSD_HEADDIM
    gw = SSD_STATE
    hp = D // SSD_GROUPS
    heads_per_group = SSD_HEADS // SSD_GROUPS

    def pack3(t):
        s1, s2, s3 = _split3(t)
        return jnp.where(lane < SSD_HEADS, s1, jnp.where(lane < 2 * SSD_HEADS, s2, s3)).astype(BF16)

    sstates = [sstate_ref[:, g * hp:(g + 1) * hp] for g in range(SSD_GROUPS)]
    rstates = [rstate_ref[hd] for hd in range(RET_HEADS)]
    y_ssd_rows, y_ret_rows = [], []
    for c in range(n_chunks):
        rows = slice(c * CHUNK, (c + 1) * CHUNK)
        dt = dt_all[rows, :]
        da = dt * a_head
        s1, s2, s3 = _split3(da)
        tri = tri_ref[...]
        acs = (jnp.dot(tri, s1.astype(BF16), preferred_element_type=F32)
               + jnp.dot(tri, s2.astype(BF16), preferred_element_type=F32)
               + jnp.dot(tri, s3.astype(BF16), preferred_element_type=F32))
        ea = jnp.exp(acs)
        wst = jnp.exp(acs[CHUNK - 1:CHUNK, :] - acs) * dt
        exp1 = jnp.dot(jnp.concatenate([pack3(ea), pack3(wst)], axis=0), e1_ref[...],
                       preferred_element_type=F32)
        ea_x = exp1[0:CHUNK, :]
        wst_x = exp1[CHUNK:2 * CHUNK, :]
        acs_t = acs.T
        dt_t = dt.T

        xs = xbc[rows, 0:D]
        bm = xbc[rows, D:D + SSD_GROUPS * gw]
        cm = xbc[rows, D + SSD_GROUPS * gw:D + 2 * SSD_GROUPS * gw]
        xs16 = xs.astype(BF16)
        xw16 = (xs * wst_x).astype(BF16)

        y_parts = []
        for g in range(SSD_GROUPS):
            bg = bm[:, g * gw:(g + 1) * gw]
            cg = cm[:, g * gw:(g + 1) * gw]
            cb = _mm_nt(cg, bg)
            y_off = _mm(cg, sstates[g]) * ea_x[:, g * hp:(g + 1) * hp]
            dec_last = ea_x[CHUNK - 1:CHUNK, g * hp:(g + 1) * hp]
            sstates[g] = sstates[g] * dec_last + _mm_tn(bg, xw16[:, g * hp:(g + 1) * hp])
            diag_parts = []
            for pr in range(heads_per_group // 2):
                ms = []
                for hh in (2 * pr, 2 * pr + 1):
                    hd = g * heads_per_group + hh
                    seg = jnp.broadcast_to(acs[:, hd:hd + 1], (CHUNK, LANES)) - acs_t[hd:hd + 1, :]
                    dec = jnp.exp(jnp.where(causal, seg, -jnp.inf))
                    ms.append((cb * dec * dt_t[hd:hd + 1, :]).astype(BF16))
                col = g * hp + pr * LANES
                xp = xs16[:, col:col + LANES]
                zero = jnp.zeros_like(xp)
                rhs = jnp.concatenate([jnp.where(lo_half, xp, zero), jnp.where(lo_half, zero, xp)], axis=0)
                diag_parts.append(jnp.dot(jnp.concatenate(ms, axis=1), rhs, preferred_element_type=F32))
            y_parts.append(jnp.concatenate(diag_parts, axis=1) + y_off)
        y = jnp.concatenate(y_parts, axis=1) + xs * dskip_ref[...]
        y = y * _silu(z[rows, :])
        y_ssd_rows.append(_rmsnorm(y, nw_ref[...]).astype(BF16))

        outs = []
        for hd in range(RET_HEADS):
            qh = qr[hd][rows, :]
            kh = kr[hd][rows, :]
            vh = v[rows, hd * RET_DV:(hd + 1) * RET_DV]
            scores = _mm_nt(qh, kh) * dmask[hd]
            lhs = jnp.concatenate([scores.astype(BF16), (qh * q_decay[hd]).astype(BF16)], axis=1)
            rhs = jnp.concatenate([vh, rstates[hd].astype(BF16)], axis=0)
            out = jnp.dot(lhs, rhs, preferred_element_type=F32)
            rstates[hd] = rstates[hd] * chunk_decay[hd] + _mm_tn(kh * k_decay[hd], vh)
            outs.append(out * lax.rsqrt(jnp.mean(out * out, axis=-1, keepdims=True) + EPS))
        y_ret_rows.append((jnp.concatenate(outs, axis=1) * _silu(gr[rows, :])).astype(BF16))
    for g in range(SSD_GROUPS):
        sstate_ref[:, g * hp:(g + 1) * hp] = sstates[g]
    for hd in range(RET_HEADS):
        rstate_ref[hd] = rstates[hd]

    merged = _sigmoid(gate_pre[1]) * br_lru
    br_ssd = jnp.dot(jnp.concatenate(y_ssd_rows, axis=0), wbr_ref[0], preferred_element_type=F32)
    merged = merged + _sigmoid(gate_pre[0]) * br_ssd
    br_ret = jnp.dot(jnp.concatenate(y_ret_rows, axis=0), wbr_ref[2], preferred_element_type=F32)
    merged = merged + _sigmoid(gate_pre[2]) * br_ret
    o_ref[...] = x + jnp.dot(merged.astype(BF16), wo_ref[...], preferred_element_type=F32)


def _mixer_call(l, x, g, w, wdt, scw, scb, dtb, alog, dskip, nw, e1, tri, lcw, lcb, wa, ba, wi, bi,
                lam, inv, wbr, wo):
    bsz, seq, _ = x.shape
    ts = min(TS_MIX, seq)
    blocks = (LRU_BLOCKS, LRU_BLOCK, LRU_BLOCK)
    ls = functools.partial(_layer_spec, l)
    return pl.pallas_call(
        _mixer_kernel,
        grid=(bsz, seq // ts),
        in_specs=[_tile_spec(ts, D), ls((1, D)), ls((P_WIDTH, D)), ls((LANES, D)),
                  ls((SSD_CONV, SSD_XBC)), ls((1, SSD_XBC)), ls((1, LANES)),
                  ls((1, LANES)), ls((1, D)), ls((1, D)),
                  _const_spec((LANES, D)), _const_spec((CHUNK, CHUNK)),
                  ls((LRU_CONV, D)), ls((1, D)), ls(blocks), ls((1, D)),
                  ls(blocks), ls((1, D)), ls((1, D)),
                  _const_spec((1, LANES)), ls((N_BRANCHES, D, D)), ls((D, D))],
        out_specs=_tile_spec(ts, D),
        out_shape=jax.ShapeDtypeStruct(x.shape, F32),
        scratch_shapes=[pltpu.VMEM((TAIL + ts, SSD_XBC), F32), pltpu.VMEM((SSD_STATE, D), F32),
                        pltpu.VMEM((TAIL + ts, D), F32), pltpu.VMEM((SUBLANES, D), F32),
                        pltpu.VMEM((RET_HEADS, RET_DK, RET_DV), F32)],
        compiler_params=_params(),
        name="mixer",
    )(x, g, w, wdt, scw, scb, dtb, alog, dskip, nw, e1, tri, lcw, lcb, wa, ba, wi, bi, lam, inv, wbr, wo)


def _head_expand_matrix():
    r = jnp.arange(LANES)[:, None]
    c = jnp.arange(D)[None, :]
    hit = (r < DT_SPLITS * SSD_HEADS) & ((c // SSD_HEADDIM) == (r % SSD_HEADS))
    return hit.astype(BF16)


def _rep_heads(v):
    rep = jnp.tile(v.astype(F32), (1, DT_SPLITS))
    return jnp.pad(rep, ((0, 0), (0, LANES - rep.shape[1])))[:, None, :]


def _dt_rows(wt):
    rows = jnp.tile(wt[:, P_DT:P_DT + SSD_HEADS, :], (1, DT_SPLITS, 1))
    return jnp.pad(rows, ((0, 0), (0, LANES - DT_SPLITS * SSD_HEADS), (0, 0)))


def kernel(x, norm_mix, w_in, ssd_conv_w, ssd_conv_b, ssd_dt_bias, ssd_a_log, ssd_d, ssd_norm,
           lru_conv_w, lru_conv_b, lru_w_a, lru_b_a, lru_w_i, lru_b_i, lru_lambda,
           w_branch, w_o, norm_ffn, ffn_w_in, ffn_conv_w, ffn_conv_b, ffn_w_out, norm_final):
    depth = w_in.shape[0]
    e1 = _head_expand_matrix()
    tri = (jnp.arange(CHUNK)[:, None] >= jnp.arange(CHUNK)[None, :]).astype(BF16)
    inv = 1.0 / (ROPE_BASE ** jnp.linspace(0.0, 1.0, RET_DK // 2, dtype=F32))
    inv_row = jnp.repeat(inv, 2)[None, :]

    rows = lambda v: v.astype(F32)[:, None, :]
    w_in_t = jnp.swapaxes(w_in, 1, 2).astype(BF16)
    mixer_args = (rows(norm_mix), w_in_t, _dt_rows(w_in_t),
                  ssd_conv_w, rows(ssd_conv_b), _rep_heads(ssd_dt_bias), _rep_heads(ssd_a_log),
                  rows(jnp.repeat(ssd_d, SSD_HEADDIM, axis=1)), rows(ssd_norm), e1, tri,
                  lru_conv_w, rows(lru_conv_b), lru_w_a.astype(BF16), rows(lru_b_a),
                  lru_w_i.astype(BF16), rows(lru_b_i), rows(lru_lambda),
                  inv_row, w_branch.astype(BF16), w_o.astype(BF16))
    ffn_args = (rows(norm_ffn), ffn_w_in.astype(BF16), ffn_conv_w, rows(ffn_conv_b),
                ffn_w_out.astype(BF16), norm_final.astype(F32)[None, :])
    for l in range(depth):
        x = _mixer_call(l, x, *mixer_args)
        x = _ffn_call(l, x, *ffn_args, final_norm=(l == depth - 1))
    return x
```
